```python
import math
import jax, jax.numpy as jnp
from jax import lax
import numpy as np

D_MODEL = 1024
BATCH = 8
SEQ = 2048
DEPTH = 1
DEC_BATCH = 128
DEC_SEQ = 8
PAST_LEN = 16384
PAGE_SIZE = 128

W_LRU = D_MODEL
LRU_BLOCKS = 8
LRU_BW = W_LRU // LRU_BLOCKS
CONV_W = 4
RG_C = 8.0
GLA_HEADS = 4
GLA_DK = (D_MODEL // 2) // GLA_HEADS
GLA_DV = D_MODEL // GLA_HEADS
GATE_RANK = 16
GATE_NORM = 16.0
CHUNK = 64
IN_SIZES = [W_LRU, W_LRU, GLA_HEADS * GLA_DK, GLA_HEADS * GLA_DK,
            GLA_HEADS * GLA_DV, GLA_HEADS * GLA_DV, GATE_RANK, 2 * D_MODEL]
N_IN = sum(IN_SIZES)
IN_SPLITS = [int(s) for s in np.cumsum(IN_SIZES)[:-1]]
EPS = 1e-6

kernel_name = "hawk_gla_parallel_gated_step"


def rms_norm(x, g):
    xf = x.astype(jnp.float32)
    y = xf * lax.rsqrt(jnp.mean(xf * xf, axis=-1, keepdims=True) + EPS)
    return (y * g.astype(jnp.float32)).astype(x.dtype)


def causal_conv(x, buf, w, b):
    T = x.shape[1]
    xp = jnp.concatenate([buf.astype(x.dtype), x], axis=1)
    y = b + xp[:, 0:T] * w[0]
    for k in range(1, CONV_W):
        y = y + xp[:, k:k + T] * w[k]
    return y, xp[:, -(CONV_W - 1):]


def rg_lru(x, h0, w_r, b_r, w_i, b_i, lam):
    B, T, W = x.shape
    xf = x.astype(jnp.float32)
    xb = xf.reshape(B, T, LRU_BLOCKS, LRU_BW)
    r = jax.nn.sigmoid(jnp.einsum('btnc,ncd->btnd', xb, w_r.astype(jnp.float32))
                       + b_r.astype(jnp.float32).reshape(LRU_BLOCKS, LRU_BW)).reshape(B, T, W)
    i = jax.nn.sigmoid(jnp.einsum('btnc,ncd->btnd', xb, w_i.astype(jnp.float32))
                       + b_i.astype(jnp.float32).reshape(LRU_BLOCKS, LRU_BW)).reshape(B, T, W)
    log_a = -RG_C * r * jax.nn.softplus(-lam.astype(jnp.float32))
    a = jnp.exp(log_a)
    u = jnp.sqrt(-jnp.expm1(2.0 * log_a)) * (i * xf)
    u = u.at[:, 0].add(a[:, 0] * h0.astype(jnp.float32))

    def comb(left, right):
        al, bl = left
        ar, br = right
        return al * ar, ar * bl + br

    _, h = lax.associative_scan(comb, (a, u), axis=1)
    return h, h[:, -1]


def gla_chunked(q, k, v, g, S0):
    B, T, H, _ = q.shape
    C = CHUNK if T >= CHUNK else T
    pad = (-T) % C
    if pad:
        padf = lambda z: jnp.pad(z, ((0, 0), (0, pad), (0, 0), (0, 0)))
        q, k, v, g = padf(q), padf(k), padf(v), padf(g)
    n = (T + pad) // C
    to_chunks = lambda z: z.reshape(B, n, C, H, z.shape[-1]).transpose(1, 0, 3, 2, 4)
    causal = jnp.tril(jnp.ones((C, C), dtype=bool))

    def step(S, inp):
        qc, kc, vc, gc = inp
        b = jnp.cumsum(gc, axis=2)
        diff = b[:, :, :, None, :] - b[:, :, None, :, :]
        decay = jnp.exp(jnp.where(causal[:, :, None], diff, -jnp.inf))
        att = jnp.einsum('bhtd,bhsd,bhtsd->bhts', qc, kc, decay)
        o = (jnp.einsum('bhts,bhsv->bhtv', att, vc)
             + jnp.einsum('bhtd,bhdv->bhtv', qc * jnp.exp(b), S))
        b_last = b[:, :, -1:, :]
        S_new = (jnp.exp(b_last[:, :, 0, :])[..., None] * S
                 + jnp.einsum('bhsd,bhsv->bhdv', kc * jnp.exp(b_last - b), vc))
        return S_new, o

    S_fin, o = lax.scan(step, S0.astype(jnp.float32),
                        (to_chunks(q), to_chunks(k), to_chunks(v), to_chunks(g)))
    o = o.transpose(1, 0, 3, 2, 4).reshape(B, n * C, H, v.shape[-1])[:, :T]
    return o, S_fin


def mixer_layer(x, h0, conv0, S0, norm_g, w_in, conv_w, conv_b, w_r, b_r, w_i, b_i, lam,
                w_a2, b_a, gla_norm_g, w_lru_o, w_gla_o, b_merge, w_out):
    B, T, _ = x.shape
    dt = x.dtype
    xn = rms_norm(x, norm_g)
    proj = jnp.einsum('btd,dn->btn', xn, w_in)
    x_lru, z_lru, q, k, v, z_gla, a_lo, gate_logits = jnp.split(proj, IN_SPLITS, axis=-1)

    xc, conv_new = causal_conv(x_lru, conv0, conv_w, conv_b)
    h, h_last = rg_lru(xc, h0, w_r, b_r, w_i, b_i, lam)
    y_a = jnp.einsum('btw,wd->btd', h.astype(dt) * jax.nn.silu(z_lru), w_lru_o)

    qf = q.astype(jnp.float32).reshape(B, T, GLA_HEADS, GLA_DK) * (GLA_DK ** -0.5)
    kf = k.astype(jnp.float32).reshape(B, T, GLA_HEADS, GLA_DK)
    vf = v.astype(jnp.float32).reshape(B, T, GLA_HEADS, GLA_DV)
    g = jax.nn.log_sigmoid((jnp.einsum('btr,rk->btk', a_lo, w_a2) + b_a).astype(jnp.float32)) / GATE_NORM
    g = g.reshape(B, T, GLA_HEADS, GLA_DK)
    o, S_new = gla_chunked(qf, kf, vf, g, S0)
    o = o * lax.rsqrt(jnp.mean(o * o, axis=-1, keepdims=True) + EPS)
    o = (o * gla_norm_g.astype(jnp.float32).reshape(GLA_HEADS, GLA_DV)).reshape(B, T, GLA_HEADS * GLA_DV)
    y_b = jnp.einsum('btv,vd->btd', o.astype(dt) * jax.nn.silu(z_gla), w_gla_o)

    gates = jax.nn.sigmoid(gate_logits + b_merge)
    g_a, g_b = jnp.split(gates, 2, axis=-1)
    merged = g_a * y_a + g_b * y_b
    out = x + jnp.einsum('btd,de->bte', merged, w_out)
    return out, h_last, conv_new, S_new


def setup_inputs(seed: int = 0) -> dict:
    key = jax.random.key(seed)
    ks = jax.random.split(key, 24)
    nrm = lambda k_, shape, s: jax.random.normal(k_, shape, jnp.float32) * s
    u = jax.random.uniform(ks[9], (DEPTH, W_LRU), jnp.float32, 0.9, 0.999)
    s = u ** (1.0 / RG_C)
    lam = jnp.log(s) - jnp.log1p(-s)
    return {
        "x_prompt": nrm(ks[0], (BATCH, SEQ, D_MODEL), 1.0),
        "x_sample": nrm(ks[1], (DEC_BATCH, DEC_SEQ, D_MODEL), 1.0),
        "state_lru_h": nrm(ks[2], (DEPTH, DEC_BATCH, W_LRU), 0.5),
        "state_lru_conv": nrm(ks[3], (DEPTH, DEC_BATCH, CONV_W - 1, W_LRU), 1.0),
        "state_gla": nrm(ks[4], (DEPTH, DEC_BATCH, GLA_HEADS, GLA_DK, GLA_DV), 0.5),
        "norm_g": 1.0 + nrm(ks[5], (DEPTH, D_MODEL), 0.01),
        "w_in": nrm(ks[6], (DEPTH, D_MODEL, N_IN), D_MODEL ** -0.5),
        "conv_w": nrm(ks[7], (DEPTH, CONV_W, W_LRU), CONV_W ** -0.5),
        "conv_b": nrm(ks[8], (DEPTH, W_LRU), 0.01),
        "w_r": nrm(ks[10], (DEPTH, LRU_BLOCKS, LRU_BW, LRU_BW), LRU_BW ** -0.5),
        "b_r": nrm(ks[11], (DEPTH, W_LRU), 0.01),
        "w_i": nrm(ks[12], (DEPTH, LRU_BLOCKS, LRU_BW, LRU_BW), LRU_BW ** -0.5),
        "b_i": nrm(ks[13], (DEPTH, W_LRU), 0.01),
        "lam": lam,
        "w_a2": nrm(ks[14], (DEPTH, GATE_RANK, GLA_HEADS * GLA_DK), GATE_RANK ** -0.5),
        "b_a": nrm(ks[15], (DEPTH, GLA_HEADS * GLA_DK), 0.01),
        "gla_norm_g": 1.0 + nrm(ks[16], (DEPTH, GLA_HEADS * GLA_DV), 0.01),
        "w_lru_o": nrm(ks[17], (DEPTH, W_LRU, D_MODEL), W_LRU ** -0.5),
        "w_gla_o": nrm(ks[18], (DEPTH, GLA_HEADS * GLA_DV, D_MODEL), (GLA_HEADS * GLA_DV) ** -0.5),
        "b_merge": nrm(ks[19], (DEPTH, 2 * D_MODEL), 0.01),
        "w_out": nrm(ks[20], (DEPTH, D_MODEL, D_MODEL), D_MODEL ** -0.5),
        "final_norm_g": 1.0 + nrm(ks[21], (D_MODEL,), 0.01),
    }


def reference(x_prompt, x_sample, state_lru_h, state_lru_conv, state_gla, norm_g, w_in, conv_w, conv_b,
              w_r, b_r, w_i, b_i, lam, w_a2, b_a, gla_norm_g, w_lru_o, w_gla_o, b_merge, w_out, final_norm_g):
    Bp = x_prompt.shape[0]
    hp, cp, sp, hs, cs, ss = [], [], [], [], [], []
    xp, xs = x_prompt, x_sample
    for l in range(DEPTH):
        w = (norm_g[l], w_in[l], conv_w[l], conv_b[l], w_r[l], b_r[l], w_i[l], b_i[l], lam[l],
             w_a2[l], b_a[l], gla_norm_g[l], w_lru_o[l], w_gla_o[l], b_merge[l], w_out[l])
        h0 = jnp.zeros((Bp, W_LRU), jnp.float32)
        c0 = jnp.zeros((Bp, CONV_W - 1, W_LRU), xp.dtype)
        S0 = jnp.zeros((Bp, GLA_HEADS, GLA_DK, GLA_DV), jnp.float32)
        xp, h_p, c_p, s_p = mixer_layer(xp, h0, c0, S0, *w)
        xs, h_s, c_s, s_s = mixer_layer(xs, state_lru_h[l], state_lru_conv[l], state_gla[l], *w)
        hp.append(h_p); cp.append(c_p); sp.append(s_p)
        hs.append(h_s); cs.append(c_s); ss.append(s_s)
    y_prompt = rms_norm(xp, final_norm_g)
    y_sample = rms_norm(xs, final_norm_g)
    return (y_prompt, y_sample, jnp.stack(hp), jnp.stack(cp), jnp.stack(sp),
            jnp.stack(hs), jnp.stack(cs), jnp.stack(ss))
```

```python
import functools

import jax
import jax.numpy as jnp
from jax import lax
from jax.experimental import pallas as pl
from jax.experimental.pallas import tpu as pltpu

D_MODEL = 1024
W_LRU = 1024
LRU_BLOCKS = 8
LRU_BW = W_LRU // LRU_BLOCKS
CONV_W = 4
RG_C = 8.0
GLA_HEADS = 4
GLA_DK = 128
GLA_DV = 256
GATE_RANK = 16
GATE_NORM = 16.0
EPS = 1e-6
IN_SIZES = (W_LRU, W_LRU, GLA_HEADS * GLA_DK, GLA_HEADS * GLA_DK,
            GLA_HEADS * GLA_DV, GLA_HEADS * GLA_DV, GATE_RANK, 2 * D_MODEL)

LANES = 128
SUBLANES = 8
PROMPT_TILE = 256
SAMPLE_SEQS = 8
VMEM_LIMIT = 56 * 1024 * 1024

F32 = jnp.float32
BF16 = jnp.bfloat16


def _dot(a, b):
    return jnp.dot(a, b, preferred_element_type=F32)


def _dot_nt(a, b):
    return lax.dot_general(a, b, (((1,), (1,)), ((), ())), preferred_element_type=F32)


def _dot_tn(a, b):
    return lax.dot_general(a, b, (((0,), (0,)), ((), ())), preferred_element_type=F32)


def _seg_bcast(arr, blk, idx):
    rows, cols = arr.shape
    bb = arr.reshape(rows // blk, blk, cols)
    return jnp.broadcast_to(bb[:, idx:idx + 1, :], bb.shape).reshape(rows, cols)


def _split3(x):
    hi = x.astype(BF16)
    r1 = x - hi.astype(F32)
    mid = r1.astype(BF16)
    lo = (r1 - mid.astype(F32)).astype(BF16)
    return hi, mid, lo


def _mixer_kernel(ns, tt, has_state, *refs):
    r_rows = ns * tt
    it = iter(refs)
    x_ref = next(it)
    if has_state:
        spad_ref = next(it)
        s0_ref = next(it)
    (ng_ref, wxl_ref, wzl_ref, wq_ref, wk_ref, wv_ref, wzg_ref, wal_ref, wgl_ref,
     cw_ref, cb_ref, wr_ref, br_ref, wi_ref, bi_ref, lam_ref, wa2_ref, ba_ref,
     gng_ref, wlo_ref, wgo_ref, bm_ref, wout_ref, fng_ref) = [next(it) for _ in range(24)]
    y_ref, htail_ref, xtail_ref, sout_ref = [next(it) for _ in range(4)]
    (xn_ref, xl_ref, h_ref, ya_ref, q_ref, k_ref, g_ref, v_ref, zg_ref,
     oi_ref, ybin_ref) = [next(it) for _ in range(11)]
    if not has_state:
        xlprev_ref = next(it)

    if not has_state:
        @pl.when(pl.program_id(1) == 0)
        def _():
            htail_ref[...] = jnp.zeros_like(htail_ref)
            sout_ref[...] = jnp.zeros_like(sout_ref)
            xlprev_ref[...] = jnp.zeros_like(xlprev_ref)

    x = x_ref[0]
    ms = jnp.mean(x * x, axis=-1, keepdims=True)
    xn_ref[...] = (x * lax.rsqrt(ms + EPS) * ng_ref[...]).astype(BF16)

    xl_ref[...] = _dot(xn_ref[...], wxl_ref[...])
    rowl = lax.broadcasted_iota(jnp.int32, (r_rows, LANES), 0)
    tpos = rowl & (tt - 1)
    for n in range(LRU_BLOCKS):
        sl = slice(n * LRU_BW, (n + 1) * LRU_BW)
        xl = xl_ref[:, sl]
        if has_state:
            prev = spad_ref[0, :, sl]
            h0 = prev
        else:
            prev = xlprev_ref[:, sl]
            h0 = htail_ref[0, SUBLANES - 1:SUBLANES, sl]
        xc = cb_ref[:, sl] + cw_ref[CONV_W - 1:CONV_W, sl] * xl
        for kk in range(1, CONV_W):
            sh = jnp.where(tpos >= kk, pltpu.roll(xl, kk, 0),
                           pltpu.roll(prev, (kk - tt) % r_rows, 0))
            xc = xc + cw_ref[CONV_W - 1 - kk:CONV_W - kk, sl] * sh
        xcb = xc.astype(BF16)
        rg = jax.nn.sigmoid(_dot(xcb, wr_ref[n]) + br_ref[:, sl])
        ig = jax.nn.sigmoid(_dot(xcb, wi_ref[n]) + bi_ref[:, sl])
        log_a = (-RG_C) * rg * jax.nn.softplus(-lam_ref[:, sl])
        a = jnp.exp(log_a)
        u = jnp.sqrt(-jnp.tanh(log_a) * (a * a + 1.0)) * (ig * xc)
        u = u + jnp.where(tpos == 0, a * h0, 0.0)
        s = 1
        while s < tt:
            m = tpos >= s
            u_sh = jnp.where(m, pltpu.roll(u, s, 0), 0.0)
            a_sh = jnp.where(m, pltpu.roll(a, s, 0), 1.0)
            u = a * u_sh + u
            a = a * a_sh
            s *= 2
        h_ref[:, sl] = u
    if not has_state:
        xlprev_ref[...] = xl_ref[...]
    if tt == SUBLANES:
        htail_ref[0] = h_ref[...]
        xtail_ref[0] = xl_ref[...]
    else:
        htail_ref[0] = h_ref[r_rows - SUBLANES:, :]
        xtail_ref[0] = xl_ref[r_rows - SUBLANES:, :]
    zl = _dot(xn_ref[...], wzl_ref[...])
    ya_ref[...] = _dot((h_ref[...] * (zl * jax.nn.sigmoid(zl))).astype(BF16), wlo_ref[...])

    q_ref[...] = _dot(xn_ref[...], wq_ref[...]) * (GLA_DK ** -0.5)
    k_ref[...] = _dot(xn_ref[...], wk_ref[...])
    v_ref[...] = _dot(xn_ref[...], wv_ref[...])
    zg_ref[...] = _dot(xn_ref[...], wzg_ref[...])
    al = _dot(xn_ref[...], wal_ref[...])
    gpre = _dot(al.astype(BF16), wa2_ref[...]) + ba_ref[...]
    g_ref[...] = jax.nn.log_sigmoid(gpre) * (1.0 / GATE_NORM)

    ti = lax.broadcasted_iota(jnp.int32, (r_rows, r_rows), 0)
    si = lax.broadcasted_iota(jnp.int32, (r_rows, r_rows), 1)
    tx = ti ^ si
    same = tx < tt
    lseg = jnp.where(si <= ti, jnp.where(same, 1.0, 0.0), 0.0).astype(BF16)
    code = jnp.where(ti == si, 0, jnp.where(si < ti, jnp.where(same, 32 - lax.clz(tx), -1), -1))

    for hh in range(GLA_HEADS):
        ks = slice(hh * GLA_DK, (hh + 1) * GLA_DK)
        vs = slice(hh * GLA_DV, (hh + 1) * GLA_DV)
        qh = q_ref[:, ks]
        kh = k_ref[:, ks]
        gh = g_ref[:, ks]
        vh = v_ref[:, vs].astype(BF16)
        g1, g2, g3 = _split3(gh)
        b = _dot(lseg, g1) + _dot(lseg, g2) + _dot(lseg, g3)
        bl = _seg_bcast(b, tt, tt - 1)

        att = jnp.where(code == 0, _dot_nt(qh.astype(BF16), kh.astype(BF16)), 0.0)
        m = 1
        lvl = 1
        while m < tt:
            if m == 1:
                dd = jnp.where((rowl & 1) == 1, gh, 0.0)
            elif m == 2:
                c4 = rowl & 3
                dd = jnp.where(c4 == 0, pltpu.roll(gh, r_rows - 1, 0),
                               jnp.where(c4 == 2, gh,
                                         jnp.where(c4 == 3, gh + pltpu.roll(gh, 1, 0), 0.0)))
            else:
                dd = -jnp.abs(b - _seg_bcast(b, 2 * m, m - 1))
            z = (jnp.where((rowl & m) != 0, qh, kh) * jnp.exp(dd)).astype(BF16)
            att = jnp.where(code == lvl, _dot_nt(z, z), att)
            m *= 2
            lvl += 1
        o_intra = _dot(att.astype(BF16), vh)

        qe = qh * jnp.exp(b)
        kl = kh * jnp.exp(bl - b)
        if ns == 1:
            bt = jnp.transpose(b[r_rows - SUBLANES:, :])
        else:
            bt = jnp.transpose(b)
        for s in range(ns):
            rs = slice(s * tt, (s + 1) * tt)
            col = SUBLANES - 1 if ns == 1 else s * tt + tt - 1
            dcol = jnp.exp(bt[:, col:col + 1])
            st = s0_ref[s, hh] if has_state else sout_ref[0, hh]
            oi_ref[rs, vs] = _dot(qe[rs].astype(BF16), st.astype(BF16))
            snew = dcol * st + _dot_tn(kl[rs].astype(BF16), vh[rs] if ns == 1 else v_ref[rs, vs].astype(BF16))
            if has_state:
                sout_ref[s, hh] = snew
            else:
                sout_ref[0, hh] = snew
        o = o_intra + oi_ref[:, vs]
        on = o * lax.rsqrt(jnp.mean(o * o, axis=-1, keepdims=True) + EPS) * gng_ref[:, vs]
        zg = zg_ref[:, vs]
        ybin_ref[:, vs] = (on.astype(F32) * (zg * jax.nn.sigmoid(zg))).astype(BF16)
    yb = _dot(ybin_ref[...], wgo_ref[...])

    gl = _dot(xn_ref[...], wgl_ref[...]) + bm_ref[...]
    gates = jax.nn.sigmoid(gl)
    merged = gates[:, :D_MODEL] * ya_ref[...] + gates[:, D_MODEL:] * yb
    out = x_ref[0] + _dot(merged.astype(BF16), wout_ref[...])
    ms2 = jnp.mean(out * out, axis=-1, keepdims=True)
    y_ref[0] = out * lax.rsqrt(ms2 + EPS) * fng_ref[...]


def _resident(shape):
    nd = len(shape)
    return pl.BlockSpec(shape, lambda *_: (0,) * nd, pipeline_mode=pl.Buffered(1))


def _run_group(x, weights, ns, tt, state=None):
    has_state = state is not None
    r_rows = ns * tt
    groups, rows_total, _ = x.shape
    nt = rows_total // r_rows
    tile = lambda c: pl.BlockSpec((1, r_rows, c), lambda i, j: (i, j, 0))
    in_specs = [tile(D_MODEL)]
    args = [x]
    if has_state:
        spad, s0 = state
        in_specs += [tile(W_LRU),
                     pl.BlockSpec((ns, GLA_HEADS, GLA_DK, GLA_DV), lambda i, j: (i, 0, 0, 0))]
        args += [spad, s0]
    in_specs += [_resident(w.shape) for w in weights]
    args += list(weights)

    tail = (1, ns * SUBLANES, W_LRU)
    s_blk = (ns, GLA_HEADS, GLA_DK, GLA_DV) if has_state else (1, GLA_HEADS, GLA_DK, GLA_DV)
    out_shape = (
        jax.ShapeDtypeStruct(x.shape, F32),
        jax.ShapeDtypeStruct((groups,) + tail[1:], F32),
        jax.ShapeDtypeStruct((groups,) + tail[1:], F32),
        jax.ShapeDtypeStruct((groups * s_blk[0],) + s_blk[1:], F32),
    )
    out_specs = (
        tile(D_MODEL),
        pl.BlockSpec(tail, lambda i, j: (i, 0, 0)),
        pl.BlockSpec(tail, lambda i, j: (i, 0, 0)),
        pl.BlockSpec(s_blk, lambda i, j: (i, 0, 0, 0)),
    )
    hk = GLA_HEADS * GLA_DK
    hv = GLA_HEADS * GLA_DV
    scratch = [
        pltpu.VMEM((r_rows, D_MODEL), BF16),
        pltpu.VMEM((r_rows, W_LRU), F32),
        pltpu.VMEM((r_rows, W_LRU), F32),
        pltpu.VMEM((r_rows, D_MODEL), F32),
        pltpu.VMEM((r_rows, hk), F32),
        pltpu.VMEM((r_rows, hk), F32),
        pltpu.VMEM((r_rows, hk), F32),
        pltpu.VMEM((r_rows, hv), F32),
        pltpu.VMEM((r_rows, hv), F32),
        pltpu.VMEM((r_rows, hv), F32),
        pltpu.VMEM((r_rows, hv), BF16),
    ]
    if not has_state:
        scratch.append(pltpu.VMEM((r_rows, W_LRU), F32))
    return pl.pallas_call(
        functools.partial(_mixer_kernel, ns, tt, has_state),
        grid=(groups, nt),
        in_specs=in_specs,
        out_specs=out_specs,
        out_shape=out_shape,
        scratch_shapes=scratch,
        compiler_params=pltpu.CompilerParams(
            dimension_semantics=("arbitrary", "arbitrary"),
            vmem_limit_bytes=VMEM_LIMIT),
        name="mixer_sample" if has_state else "mixer_prompt",
    )(*args)


def kernel(x_prompt, x_sample, state_lru_h, state_lru_conv, state_gla, norm_g, w_in, conv_w, conv_b,
           w_r, b_r, w_i, b_i, lam, w_a2, b_a, gla_norm_g, w_lru_o, w_gla_o, b_merge, w_out, final_norm_g):
    assert norm_g.shape[0] == 1, "single-layer trunk"
    bp, seq, _ = x_prompt.shape
    bs, dseq, _ = x_sample.shape
    assert seq % PROMPT_TILE == 0 and dseq == SUBLANES and bs % SAMPLE_SEQS == 0

    row = lambda p: p[0].reshape(1, -1).astype(F32)
    offs = [0]
    for sz in IN_SIZES:
        offs.append(offs[-1] + sz)
    win = w_in[0]
    piece = lambda i: win[:, offs[i]:offs[i + 1]].astype(BF16)
    w_al = jnp.pad(piece(6), ((0, 0), (0, LANES - GATE_RANK)))
    w_a2p = jnp.pad(w_a2[0].astype(BF16), ((0, LANES - GATE_RANK), (0, 0)))
    weights = (
        row(norm_g), piece(0), piece(1), piece(2), piece(3), piece(4), piece(5), w_al, piece(7),
        conv_w[0].astype(F32), row(conv_b), w_r[0].astype(BF16), row(b_r), w_i[0].astype(BF16), row(b_i),
        row(lam), w_a2p, row(b_a), row(gla_norm_g), w_lru_o[0].astype(BF16), w_gla_o[0].astype(BF16),
        row(b_merge), w_out[0].astype(BF16), final_norm_g.reshape(1, -1).astype(F32),
    )

    yp, hp, cp, sp = _run_group(x_prompt, weights, 1, PROMPT_TILE)

    ns = SAMPLE_SEQS
    spad = jnp.concatenate(
        [state_lru_h[0][:, None, :],
         jnp.zeros((bs, SUBLANES - CONV_W, W_LRU), F32),
         state_lru_conv[0].astype(F32)], axis=1)
    ys, hs, cs, ss = _run_group(
        x_sample.reshape(bs // ns, ns * dseq, D_MODEL), weights, ns, dseq,
        state=(spad.reshape(bs // ns, ns * dseq, W_LRU), state_gla[0]))

    tailp = lambda t: t.reshape(bp, SUBLANES, W_LRU)
    tails = lambda t: t.reshape(bs, SUBLANES, W_LRU)
    keep = SUBLANES - (CONV_W - 1)
    return (yp, ys.reshape(bs, dseq, D_MODEL),
            tailp(hp)[None, :, SUBLANES - 1], tailp(cp)[None, :, keep:], sp[None],
            tails(hs)[None, :, SUBLANES - 1], tails(cs)[None, :, keep:], ss[None])
```

```python
import functools
import math

import jax
import jax.numpy as jnp
from jax import lax
from jax.experimental import pallas as pl
from jax.experimental.pallas import tpu as pltpu

D_MODEL = 1024
W_LRU = 1024
LRU_BLOCKS = 8
LRU_BW = W_LRU // LRU_BLOCKS
CONV_W = 4
RG_C = 8.0
GLA_HEADS = 4
GLA_DK = 128
GLA_DV = 256
GATE_RANK = 16
GATE_NORM = 16.0
EPS = 1e-6
IN_SIZES = (W_LRU, W_LRU, GLA_HEADS * GLA_DK, GLA_HEADS * GLA_DK,
            GLA_HEADS * GLA_DV, GLA_HEADS * GLA_DV, GATE_RANK, 2 * D_MODEL)

LANES = 128
SUBLANES = 8
PROMPT_TILE = 256
SAMPLE_SEQS = 8
VMEM_LIMIT = 56 * 1024 * 1024
LOG2E = math.log2(math.e)

F32 = jnp.float32
BF16 = jnp.bfloat16


def _dot(a, b):
    return jnp.dot(a, b, preferred_element_type=F32)


def _dot_nt(a, b):
    return lax.dot_general(a, b, (((1,), (1,)), ((), ())), preferred_element_type=F32)


def _dot_tn(a, b):
    return lax.dot_general(a, b, (((0,), (0,)), ((), ())), preferred_element_type=F32)


def _seg_bcast(arr, blk, idx):
    rows, cols = arr.shape
    bb = arr.reshape(rows // blk, blk, cols)
    return jnp.broadcast_to(bb[:, idx:idx + 1, :], bb.shape).reshape(rows, cols)


def _split3(x):
    hi = x.astype(BF16)
    r1 = x - hi.astype(F32)
    mid = r1.astype(BF16)
    lo = (r1 - mid.astype(F32)).astype(BF16)
    return hi, mid, lo


def _lru_gates(xc, n, wr_ref, br_ref, wi_ref, bi_ref, lam_ref):
    sl = slice(n * LRU_BW, (n + 1) * LRU_BW)
    xcb = xc.astype(BF16)
    rg = jax.nn.sigmoid(_dot(xcb, wr_ref[n]) + br_ref[:, sl])
    ig = jax.nn.sigmoid(_dot(xcb, wi_ref[n]) + bi_ref[:, sl])
    log_a = (-RG_C) * rg * jax.nn.softplus(-lam_ref[:, sl])
    a = jnp.exp(log_a)
    u = jnp.sqrt(-jnp.tanh(log_a) * (a * a + 1.0)) * (ig * xc)
    return a, u


def _scan_sublanes(a, u, pos, length):
    s = 1
    while s < length:
        m = pos >= s
        u_sh = jnp.where(m, pltpu.roll(u, s, 0), 0.0)
        a_sh = jnp.where(m, pltpu.roll(a, s, 0), 1.0)
        u = a * u_sh + u
        a = a * a_sh
        s *= 2
    return a, u


def _mixer_kernel(ns, tt, has_state, *refs):
    r_rows = ns * tt
    it = iter(refs)
    x_ref = next(it)
    if has_state:
        spad_ref = next(it)
        s0_ref = next(it)
    (ng_ref, wxl_ref, wzl_ref, wq_ref, wk_ref, wv_ref, wzg_ref, wal_ref, wgl_ref,
     cw_ref, cb_ref, wr_ref, br_ref, wi_ref, bi_ref, lam_ref, wa2_ref, ba_ref,
     gng_ref, wlo_ref, wgo_ref, bm_ref, wout_ref, fng_ref) = [next(it) for _ in range(24)]
    y_ref, htail_ref, xtail_ref, sout_ref = [next(it) for _ in range(4)]
    (xn_ref, xl_ref, h_ref, zl_ref, q_ref, k_ref, g_ref, v_ref, zg_ref,
     gl_ref, oi_ref, ybin_ref) = [next(it) for _ in range(12)]

    if not has_state:
        @pl.when(pl.program_id(1) == 0)
        def _():
            htail_ref[...] = jnp.zeros_like(htail_ref)
            xtail_ref[...] = jnp.zeros_like(xtail_ref)
            sout_ref[...] = jnp.zeros_like(sout_ref)

    x = x_ref[0]
    ms = jnp.mean(x * x, axis=-1, keepdims=True)
    xn_ref[...] = (x * lax.rsqrt(ms + EPS) * ng_ref[...]).astype(BF16)

    def proj_zl():
        zl_ref[...] = _dot(xn_ref[...], wzl_ref[...])

    def proj_qk():
        q_ref[...] = _dot(xn_ref[...], wq_ref[...]) * (GLA_DK ** -0.5)
        k_ref[...] = _dot(xn_ref[...], wk_ref[...])

    def proj_v():
        v_ref[...] = _dot(xn_ref[...], wv_ref[...]).astype(BF16)

    def proj_zg():
        zg_ref[...] = _dot(xn_ref[...], wzg_ref[...])

    def proj_g():
        al = _dot(xn_ref[...], wal_ref[...])
        gpre = _dot(al.astype(BF16), wa2_ref[...]) + ba_ref[...]
        g_ref[...] = jax.nn.log_sigmoid(gpre) * (LOG2E / GATE_NORM)

    def proj_gl(half):
        cs = slice(half * D_MODEL, (half + 1) * D_MODEL)
        gl_ref[:, cs] = _dot(xn_ref[...], wgl_ref[:, cs]) + bm_ref[:, cs]

    jobs = [proj_zl, proj_qk, proj_v, proj_zg, proj_g,
            functools.partial(proj_gl, 0), functools.partial(proj_gl, 1), None]

    xl_full = _dot(xn_ref[...], wxl_ref[...])
    for n in range(LRU_BLOCKS):
        xl_ref[n] = xl_full[:, n * LRU_BW:(n + 1) * LRU_BW]
    sub = lax.broadcasted_iota(jnp.int32, (SUBLANES, LANES), 0)
    for n in range(LRU_BLOCKS):
        sl = slice(n * LRU_BW, (n + 1) * LRU_BW)
        cw = [cw_ref[j:j + 1, sl] for j in range(CONV_W)]
        if has_state:
            rowl = lax.broadcasted_iota(jnp.int32, (r_rows, LANES), 0)
            tpos = rowl & (tt - 1)
            xl = xl_ref[n]
            prev = spad_ref[0, :, sl]
            xc = cb_ref[:, sl] + cw[CONV_W - 1] * xl
            for kk in range(1, CONV_W):
                sh = jnp.where(tpos >= kk, pltpu.roll(xl, kk, 0),
                               pltpu.roll(prev, (kk - tt) % r_rows, 0))
                xc = xc + cw[CONV_W - 1 - kk] * sh
            a, u = _lru_gates(xc, n, wr_ref, br_ref, wi_ref, bi_ref, lam_ref)
            u = u + jnp.where(tpos == 0, a * prev, 0.0)
            _, h = _scan_sublanes(a, u, tpos, tt)
            h_ref[n] = h
            htail_ref[0, :, sl] = h
            xtail_ref[0, :, sl] = xl
        else:
            nv = r_rows // SUBLANES
            xv = [xl_ref[n, pl.ds(v, SUBLANES, stride=nv), :] for v in range(nv)]
            xtail = xtail_ref[0, :, sl]
            hprev = htail_ref[0, SUBLANES - 1:SUBLANES, sl]

            def shifted(v, kk):
                if v >= kk:
                    return xv[v - kk]
                r = SUBLANES + v - kk
                return jnp.where(sub == 0, xtail[r:r + 1, :], pltpu.roll(xv[nv + v - kk], 1, 0))

            xc = jnp.concatenate(
                [cb_ref[:, sl] + cw[CONV_W - 1] * xv[v]
                 + sum(cw[CONV_W - 1 - kk] * shifted(v, kk) for kk in range(1, CONV_W))
                 for v in range(nv)], axis=0)
            a, u = _lru_gates(xc, n, wr_ref, br_ref, wi_ref, bi_ref, lam_ref)
            av = [a[v * SUBLANES:(v + 1) * SUBLANES] for v in range(nv)]
            uv = [u[v * SUBLANES:(v + 1) * SUBLANES] for v in range(nv)]
            hl = [uv[0] + jnp.where(sub == 0, av[0] * hprev, 0.0)]
            pr = [av[0]]
            for v in range(1, nv):
                hl.append(av[v] * hl[-1] + uv[v])
                pr.append(av[v] * pr[-1])
            _, cend = _scan_sublanes(pr[-1], hl[-1], sub, SUBLANES)
            cin = jnp.where(sub >= 1, pltpu.roll(cend, 1, 0), 0.0)
            for v in range(nv):
                h_ref[n, pl.ds(v, SUBLANES, stride=nv), :] = hl[v] + pr[v] * cin
            htail_ref[0, :, sl] = h_ref[n, r_rows - SUBLANES:, :]
            xtail_ref[0, :, sl] = xl_ref[n, r_rows - SUBLANES:, :]
        if jobs[n] is not None:
            jobs[n]()

    zl = zl_ref[...]
    hfull = jnp.concatenate([h_ref[n] for n in range(LRU_BLOCKS)], axis=1)
    ya = _dot((hfull * (zl * jax.nn.sigmoid(zl))).astype(BF16), wlo_ref[...])

    ti = lax.broadcasted_iota(jnp.int32, (r_rows, r_rows), 0)
    si = lax.broadcasted_iota(jnp.int32, (r_rows, r_rows), 1)
    tx = ti ^ si
    same = tx < tt
    lseg = jnp.where(si <= ti, jnp.where(same, 1.0, 0.0), 0.0).astype(BF16)
    code = jnp.where(ti == si, 0, jnp.where(si < ti, jnp.where(same, 32 - lax.clz(tx), -1), -1))
    rowl = lax.broadcasted_iota(jnp.int32, (r_rows, LANES), 0)

    for hh in range(GLA_HEADS):
        ks = slice(hh * GLA_DK, (hh + 1) * GLA_DK)
        vs = slice(hh * GLA_DV, (hh + 1) * GLA_DV)
        qh = q_ref[:, ks]
        kh = k_ref[:, ks]
        gh = g_ref[:, ks]
        vh = v_ref[:, vs]
        g1, g2, g3 = _split3(gh)
        b = _dot(lseg, g1) + _dot(lseg, g2) + _dot(lseg, g3)
        bl = _seg_bcast(b, tt, tt - 1)

        att = jnp.where(code == 0, _dot_nt(qh.astype(BF16), kh.astype(BF16)), 0.0)
        m = 1
        lvl = 1
        while m < tt:
            if m == 1:
                dd = jnp.where((rowl & 1) == 1, gh, 0.0)
            elif m == 2:
                c4 = rowl & 3
                dd = jnp.where(c4 == 0, pltpu.roll(gh, r_rows - 1, 0),
                               jnp.where(c4 == 2, gh,
                                         jnp.where(c4 == 3, gh + pltpu.roll(gh, 1, 0), 0.0)))
            else:
                dd = -jnp.abs(b - _seg_bcast(b, 2 * m, m - 1))
            z = (jnp.where((rowl & m) != 0, qh, kh) * jnp.exp2(dd)).astype(BF16)
            att = jnp.where(code == lvl, _dot_nt(z, z), att)
            m *= 2
            lvl += 1
        o_intra = _dot(att.astype(BF16), vh)

        qe = qh * jnp.exp2(b)
        kl = kh * jnp.exp2(bl - b)
        if ns == 1:
            bt = jnp.transpose(b[r_rows - SUBLANES:, :])
        else:
            bt = jnp.transpose(b)
        for s in range(ns):
            rs = slice(s * tt, (s + 1) * tt)
            col = SUBLANES - 1 if ns == 1 else s * tt + tt - 1
            dcol = jnp.exp2(bt[:, col:col + 1])
            st = s0_ref[s, hh] if has_state else sout_ref[0, hh]
            oi_ref[rs, vs] = _dot(qe[rs].astype(BF16), st.astype(BF16))
            vrows = vh if ns == 1 else v_ref[:, vs].astype(F32)[rs].astype(BF16)
            snew = dcol * st + _dot_tn(kl[rs].astype(BF16), vrows)
            if has_state:
                sout_ref[s, hh] = snew
            else:
                sout_ref[0, hh] = snew
        o = o_intra + oi_ref[:, vs]
        on = o * lax.rsqrt(jnp.mean(o * o, axis=-1, keepdims=True) + EPS) * gng_ref[:, vs]
        zg = zg_ref[:, vs]
        ybin_ref[:, vs] = (on * (zg * jax.nn.sigmoid(zg))).astype(BF16)
    yb = _dot(ybin_ref[...], wgo_ref[...])

    gates = jax.nn.sigmoid(gl_ref[...])
    merged = gates[:, :D_MODEL] * ya + gates[:, D_MODEL:] * yb
    out = x_ref[0] + _dot(merged.astype(BF16), wout_ref[...])
    ms2 = jnp.mean(out * out, axis=-1, keepdims=True)
    y_ref[0] = out * lax.rsqrt(ms2 + EPS) * fng_ref[...]


def _resident(shape):
    nd = len(shape)
    return pl.BlockSpec(shape, lambda *_: (0,) * nd, pipeline_mode=pl.Buffered(1))


def _run_group(x, weights, ns, tt, state=None):
    has_state = state is not None
    r_rows = ns * tt
    groups, rows_total, _ = x.shape
    nt = rows_total // r_rows
    tile = lambda c: pl.BlockSpec((1, r_rows, c), lambda i, j: (i, j, 0))
    in_specs = [tile(D_MODEL)]
    args = [x]
    if has_state:
        spad, s0 = state
        in_specs += [tile(W_LRU),
                     pl.BlockSpec((ns, GLA_HEADS, GLA_DK, GLA_DV), lambda i, j: (i, 0, 0, 0))]
        args += [spad, s0]
    in_specs += [_resident(w.shape) for w in weights]
    args += list(weights)

    tail = (1, ns * SUBLANES, W_LRU)
    s_blk = (ns, GLA_HEADS, GLA_DK, GLA_DV) if has_state else (1, GLA_HEADS, GLA_DK, GLA_DV)
    out_shape = (
        jax.ShapeDtypeStruct(x.shape, F32),
        jax.ShapeDtypeStruct((groups,) + tail[1:], F32),
        jax.ShapeDtypeStruct((groups,) + tail[1:], F32),
        jax.ShapeDtypeStruct((groups * s_blk[0],) + s_blk[1:], F32),
    )
    out_specs = (
        tile(D_MODEL),
        pl.BlockSpec(tail, lambda i, j: (i, 0, 0)),
        pl.BlockSpec(tail, lambda i, j: (i, 0, 0)),
        pl.BlockSpec(s_blk, lambda i, j: (i, 0, 0, 0)),
    )
    hk = GLA_HEADS * GLA_DK
    hv = GLA_HEADS * GLA_DV
    scratch = [
        pltpu.VMEM((r_rows, D_MODEL), BF16),
        pltpu.VMEM((LRU_BLOCKS, r_rows, LRU_BW), F32),
        pltpu.VMEM((LRU_BLOCKS, r_rows, LRU_BW), F32),
        pltpu.VMEM((r_rows, W_LRU), F32),
        pltpu.VMEM((r_rows, hk), F32),
        pltpu.VMEM((r_rows, hk), F32),
        pltpu.VMEM((r_rows, hk), F32),
        pltpu.VMEM((r_rows, hv), BF16),
        pltpu.VMEM((r_rows, hv), F32),
        pltpu.VMEM((r_rows, 2 * D_MODEL), F32),
        pltpu.VMEM((r_rows, hv), F32),
        pltpu.VMEM((r_rows, hv), BF16),
    ]
    return pl.pallas_call(
        functools.partial(_mixer_kernel, ns, tt, has_state),
        grid=(groups, nt),
        in_specs=in_specs,
        out_specs=out_specs,
        out_shape=out_shape,
        scratch_shapes=scratch,
        compiler_params=pltpu.CompilerParams(
            dimension_semantics=("arbitrary", "arbitrary"),
            vmem_limit_bytes=VMEM_LIMIT),
        name="mixer_sample" if has_state else "mixer_prompt",
    )(*args)


def kernel(x_prompt, x_sample, state_lru_h, state_lru_conv, state_gla, norm_g, w_in, conv_w, conv_b,
           w_r, b_r, w_i, b_i, lam, w_a2, b_a, gla_norm_g, w_lru_o, w_gla_o, b_merge, w_out, final_norm_g):
    assert norm_g.shape[0] == 1, "single-layer trunk"
    bp, seq, _ = x_prompt.shape
    bs, dseq, _ = x_sample.shape
    assert seq % PROMPT_TILE == 0 and dseq == SUBLANES and bs % SAMPLE_SEQS == 0

    row = lambda p: p[0].reshape(1, -1).astype(F32)
    offs = [0]
    for sz in IN_SIZES:
        offs.append(offs[-1] + sz)
    win = w_in[0]
    piece = lambda i: win[:, offs[i]:offs[i + 1]].astype(BF16)
    w_al = jnp.pad(piece(6), ((0, 0), (0, LANES - GATE_RANK)))
    w_a2p = jnp.pad(w_a2[0].astype(BF16), ((0, LANES - GATE_RANK), (0, 0)))
    weights = (
        row(norm_g), piece(0), piece(1), piece(2), piece(3), piece(4), piece(5), w_al, piece(7),
        conv_w[0].astype(F32), row(conv_b), w_r[0].astype(BF16), row(b_r), w_i[0].astype(BF16), row(b_i),
        row(lam), w_a2p, row(b_a), row(gla_norm_g), w_lru_o[0].astype(BF16), w_gla_o[0].astype(BF16),
        row(b_merge), w_out[0].astype(BF16), final_norm_g.reshape(1, -1).astype(F32),
    )

    yp, hp, cp, sp = _run_group(x_prompt, weights, 1, PROMPT_TILE)

    ns = SAMPLE_SEQS
    spad = jnp.concatenate(
        [state_lru_h[0][:, None, :],
         jnp.zeros((bs, SUBLANES - CONV_W, W_LRU), F32),
         state_lru_conv[0].astype(F32)], axis=1)
    ys, hs, cs, ss = _run_group(
        x_sample.reshape(bs // ns, ns * dseq, D_MODEL), weights, ns, dseq,
        state=(spad.reshape(bs // ns, ns * dseq, W_LRU), state_gla[0]))

    tailp = lambda t: t.reshape(bp, SUBLANES, W_LRU)
    tails = lambda t: t.reshape(bs, SUBLANES, W_LRU)
    keep = SUBLANES - (CONV_W - 1)
    return (yp, ys.reshape(bs, dseq, D_MODEL),
            tailp(hp)[None, :, SUBLANES - 1], tailp(cp)[None, :, keep:], sp[None],
            tails(hs)[None, :, SUBLANES - 1], tails(cs)[None, :, keep:], ss[None])
```

```python
import functools
import math

import jax
import jax.numpy as jnp
from jax import lax
from jax.experimental import pallas as pl
from jax.experimental.pallas import tpu as pltpu

D_MODEL = 1024
W_LRU = 1024
LRU_BLOCKS = 8
LRU_BW = W_LRU // LRU_BLOCKS
CONV_W = 4
RG_C = 8.0
GLA_HEADS = 4
GLA_DK = 128
GLA_DV = 256
GATE_RANK = 16
GATE_NORM = 16.0
EPS = 1e-6
IN_SIZES = (W_LRU, W_LRU, GLA_HEADS * GLA_DK, GLA_HEADS * GLA_DK,
            GLA_HEADS * GLA_DV, GLA_HEADS * GLA_DV, GATE_RANK, 2 * D_MODEL)

LANES = 128
SUBLANES = 8
STEP = 4
WINDOW = SUBLANES * STEP
PROMPT_TILE = 256
SAMPLE_SEQS = 8
VMEM_LIMIT = 56 * 1024 * 1024
LOG2E = math.log2(math.e)

F32 = jnp.float32
BF16 = jnp.bfloat16


def _dot(a, b):
    return jnp.dot(a, b, preferred_element_type=F32)


def _dot_nt(a, b):
    return lax.dot_general(a, b, (((1,), (1,)), ((), ())), preferred_element_type=F32)


def _dot_tn(a, b):
    return lax.dot_general(a, b, (((0,), (0,)), ((), ())), preferred_element_type=F32)


def _seg_bcast(arr, blk, idx):
    rows, cols = arr.shape
    bb = arr.reshape(rows // blk, blk, cols)
    return jnp.broadcast_to(bb[:, idx:idx + 1, :], bb.shape).reshape(rows, cols)


def _split3(x):
    hi = x.astype(BF16)
    r1 = x - hi.astype(F32)
    mid = r1.astype(BF16)
    lo = (r1 - mid.astype(F32)).astype(BF16)
    return hi, mid, lo


def _lru_gates(xc, n, wri_ref, br_ref, bi_ref, lam_ref):
    sl = slice(n * LRU_BW, (n + 1) * LRU_BW)
    pre = _dot(xc.astype(BF16), wri_ref[n])
    rg = jax.nn.sigmoid(pre[:, :LRU_BW] + br_ref[:, sl])
    ig = jax.nn.sigmoid(pre[:, LRU_BW:] + bi_ref[:, sl])
    log_a = (-RG_C) * rg * jax.nn.softplus(-lam_ref[:, sl])
    a = jnp.exp(log_a)
    u = jnp.sqrt(-jnp.tanh(log_a) * (a * a + 1.0)) * (ig * xc)
    return a, u


def _scan_sublanes(a, u, pos, length):
    s = 1
    while s < length:
        m = pos >= s
        u_sh = jnp.where(m, pltpu.roll(u, s, 0), 0.0)
        a_sh = jnp.where(m, pltpu.roll(a, s, 0), 1.0)
        u = a * u_sh + u
        a = a * a_sh
        s *= 2
    return a, u


def _mixer_kernel(ns, tt, has_state, *refs):
    r_rows = ns * tt
    it = iter(refs)
    x_ref = next(it)
    if has_state:
        spad_ref = next(it)
        s0_ref = next(it)
    (ng_ref, wxl_ref, wzl_ref, wq_ref, wk_ref, wv_ref, wzg_ref, wal_ref, wgl_ref,
     cw_ref, cb_ref, wri_ref, br_ref, bi_ref, lam_ref, wa2_ref, ba_ref,
     gng_ref, wlo_ref, wgo_ref, bm_ref, wout_ref, fng_ref) = [next(it) for _ in range(23)]
    y_ref, htail_ref, xtail_ref, sout_ref = [next(it) for _ in range(4)]
    (xn_ref, xl_ref, h_ref, zl_ref, q_ref, k_ref, g_ref, b_ref, v_ref, zg_ref,
     gl_ref, oi_ref, ybin_ref, lseg_ref, code_ref, att_ref) = [next(it) for _ in range(16)]

    if not has_state:
        @pl.when(pl.program_id(1) == 0)
        def _():
            htail_ref[...] = jnp.zeros_like(htail_ref)
            xtail_ref[...] = jnp.zeros_like(xtail_ref)
            sout_ref[...] = jnp.zeros_like(sout_ref)

    @pl.when((pl.program_id(0) == 0) & (pl.program_id(1) == 0))
    def _():
        ti = lax.broadcasted_iota(jnp.int32, (r_rows, r_rows), 0)
        si = lax.broadcasted_iota(jnp.int32, (r_rows, r_rows), 1)
        tx = ti ^ si
        same = tx < tt
        lseg_ref[...] = jnp.where(si <= ti, jnp.where(same, 1.0, 0.0), 0.0).astype(BF16)
        code_ref[...] = jnp.where(si < ti, jnp.where(same, 32 - lax.clz(tx), 0), 0)

    x = x_ref[0]
    ms = jnp.mean(x * x, axis=-1, keepdims=True)
    xn_ref[...] = (x * lax.rsqrt(ms + EPS) * ng_ref[...]).astype(BF16)

    def proj_zl():
        zl_ref[...] = _dot(xn_ref[...], wzl_ref[...])

    def proj_qk():
        q_ref[...] = _dot(xn_ref[...], wq_ref[...]) * (GLA_DK ** -0.5)
        k_ref[...] = _dot(xn_ref[...], wk_ref[...])

    def proj_v():
        v_ref[...] = _dot(xn_ref[...], wv_ref[...]).astype(BF16)

    def proj_zg():
        zg_ref[...] = _dot(xn_ref[...], wzg_ref[...])

    def proj_g():
        al = _dot(xn_ref[...], wal_ref[...])
        gpre = _dot(al.astype(BF16), wa2_ref[...]) + ba_ref[...]
        g = jax.nn.log_sigmoid(gpre) * (LOG2E / GATE_NORM)
        g_ref[...] = g
        g1, g2, g3 = _split3(g)
        lseg = lseg_ref[...]
        b_ref[...] = _dot(lseg, g1) + _dot(lseg, g2) + _dot(lseg, g3)

    def proj_gl(half):
        cs = slice(half * D_MODEL, (half + 1) * D_MODEL)
        gl_ref[:, cs] = _dot(xn_ref[...], wgl_ref[:, cs]) + bm_ref[:, cs]

    jobs = [proj_zl, proj_qk, proj_v, proj_zg, proj_g,
            functools.partial(proj_gl, 0), functools.partial(proj_gl, 1), None]

    xl_full = _dot(xn_ref[...], wxl_ref[...])
    for n in range(LRU_BLOCKS):
        xl_ref[n] = xl_full[:, n * LRU_BW:(n + 1) * LRU_BW]
    sub = lax.broadcasted_iota(jnp.int32, (SUBLANES, LANES), 0)
    for n in range(LRU_BLOCKS):
        sl = slice(n * LRU_BW, (n + 1) * LRU_BW)
        cw = [cw_ref[j:j + 1, sl] for j in range(CONV_W)]
        if has_state:
            rowl = lax.broadcasted_iota(jnp.int32, (r_rows, LANES), 0)
            tpos = rowl & (tt - 1)
            xl = xl_ref[n]
            prev = spad_ref[0, :, sl]
            xc = cb_ref[:, sl] + cw[CONV_W - 1] * xl
            for kk in range(1, CONV_W):
                sh = jnp.where(tpos >= kk, pltpu.roll(xl, kk, 0),
                               pltpu.roll(prev, (kk - tt) % r_rows, 0))
                xc = xc + cw[CONV_W - 1 - kk] * sh
            a, u = _lru_gates(xc, n, wri_ref, br_ref, bi_ref, lam_ref)
            u = u + jnp.where(tpos == 0, a * prev, 0.0)
            _, h = _scan_sublanes(a, u, tpos, tt)
            h_ref[n] = h
            htail_ref[0, :, sl] = h
            xtail_ref[0, :, sl] = xl
        else:
            nw = r_rows // WINDOW
            xv = [[xl_ref[n, pl.ds(w * WINDOW + r, SUBLANES, stride=STEP), :] for r in range(STEP)]
                  for w in range(nw)]
            xtail = xtail_ref[0, :, sl]
            hprev = htail_ref[0, SUBLANES - 1:SUBLANES, sl]
            rolled = [[None] + [pltpu.roll(xv[w][q], 1, 0) for q in range(1, STEP)] for w in range(nw)]

            def shifted(w, r, kk):
                if r >= kk:
                    return xv[w][r - kk]
                q = r - kk + STEP
                before = rolled[w - 1][q] if w else xtail[SUBLANES + r - kk:SUBLANES + r - kk + 1, :]
                return jnp.where(sub == 0, before, rolled[w][q])

            xc = jnp.concatenate(
                [cb_ref[:, sl] + cw[CONV_W - 1] * xv[w][r]
                 + sum(cw[CONV_W - 1 - kk] * shifted(w, r, kk) for kk in range(1, CONV_W))
                 for w in range(nw) for r in range(STEP)], axis=0)
            a, u = _lru_gates(xc, n, wri_ref, br_ref, bi_ref, lam_ref)
            cprev = jnp.zeros((SUBLANES, LANES), F32)
            for w in range(nw):
                base = w * WINDOW
                av = [a[base + r * SUBLANES:base + (r + 1) * SUBLANES] for r in range(STEP)]
                uv = [u[base + r * SUBLANES:base + (r + 1) * SUBLANES] for r in range(STEP)]
                if w == 0:
                    uv[0] = uv[0] + jnp.where(sub == 0, av[0] * hprev, 0.0)
                hl, pr = [uv[0]], [av[0]]
                for r in range(1, STEP):
                    hl.append(av[r] * hl[-1] + uv[r])
                    pr.append(av[r] * pr[-1])
                pc, hc = _scan_sublanes(pr[-1], hl[-1], sub, SUBLANES)
                cend = hc + pc * cprev
                cin = jnp.where(sub == 0, cprev, pltpu.roll(cend, 1, 0))
                cprev = jnp.broadcast_to(cend[SUBLANES - 1:, :], (SUBLANES, LANES))
                for r in range(STEP):
                    h_ref[n, pl.ds(base + r, SUBLANES, stride=STEP), :] = hl[r] + pr[r] * cin
            htail_ref[0, :, sl] = h_ref[n, r_rows - SUBLANES:, :]
            xtail_ref[0, :, sl] = xl_ref[n, r_rows - SUBLANES:, :]
        if jobs[n] is not None:
            jobs[n]()

    zl = zl_ref[...]
    hfull = jnp.concatenate([h_ref[n] for n in range(LRU_BLOCKS)], axis=1)
    ya = _dot((hfull * (zl * jax.nn.sigmoid(zl))).astype(BF16), wlo_ref[...])

    rowl = lax.broadcasted_iota(jnp.int32, (r_rows, LANES), 0)
    dblk = min(r_rows, LANES)

    for hh in range(GLA_HEADS):
        ks = slice(hh * GLA_DK, (hh + 1) * GLA_DK)
        vs = slice(hh * GLA_DV, (hh + 1) * GLA_DV)
        qh = q_ref[:, ks]
        kh = k_ref[:, ks]
        gh = g_ref[:, ks]
        vh = v_ref[:, vs]
        b = b_ref[:, ks]
        bl = _seg_bcast(b, tt, tt - 1)

        zlow = []
        m = 1
        while m < min(tt, SUBLANES):
            if m == 1:
                dd = jnp.where((rowl & 1) == 1, gh, 0.0)
            elif m == 2:
                c4 = rowl & 3
                dd = jnp.where(c4 == 0, pltpu.roll(gh, r_rows - 1, 0),
                               jnp.where(c4 == 2, gh,
                                         jnp.where(c4 == 3, gh + pltpu.roll(gh, 1, 0), 0.0)))
            else:
                dd = -jnp.abs(b - _seg_bcast(b, 2 * m, m - 1))
            zlow.append((jnp.where((rowl & m) != 0, qh, kh) * jnp.exp2(dd)).astype(BF16))
            m *= 2
        for i in range(r_rows // dblk):
            rs = slice(i * dblk, (i + 1) * dblk)
            codeb = code_ref[rs, rs]
            attb = jnp.zeros((dblk, dblk), F32)
            for lvl, z in enumerate(zlow, start=1):
                attb = jnp.where(codeb == lvl, _dot_nt(z[rs], z[rs]), attb)
            att_ref[rs, rs] = attb
            for j in range(i + 1, r_rows // dblk):
                att_ref[rs, j * dblk:(j + 1) * dblk] = jnp.zeros((dblk, dblk), F32)
        lvl = len(zlow) + 1
        while m < tt:
            lows, ups = [], []
            for blk in range(r_rows // (2 * m)):
                r0 = blk * 2 * m
                bref = b[r0 + m - 1:r0 + m, :]
                lows.append(kh[r0:r0 + m] * jnp.exp2(bref - b[r0:r0 + m]))
                ups.append(qh[r0 + m:r0 + 2 * m] * jnp.exp2(b[r0 + m:r0 + 2 * m] - bref))
            z = jnp.concatenate([p for pair in zip(lows, ups) for p in pair], axis=0).astype(BF16)
            zup = jnp.concatenate(ups, axis=0).astype(BF16)
            prod = _dot_nt(zup, z)
            for blk in range(r_rows // (2 * m)):
                r0 = blk * 2 * m
                cv = (r0 // dblk) * dblk
                rr, cc = slice(r0 + m, r0 + 2 * m), slice(cv, cv + dblk)
                sub = prod[blk * m:(blk + 1) * m, cc]
                if m == dblk:
                    att_ref[rr, cc] = sub
                else:
                    att_ref[rr, cc] = jnp.where(code_ref[rr, cc] == lvl, sub, att_ref[rr, cc])
            m *= 2
            lvl += 1
        dg = jnp.sum(qh * kh, axis=-1, keepdims=True)
        o_intra = _dot(att_ref[...].astype(BF16), vh) + dg * vh.astype(F32)

        qe = qh * jnp.exp2(b)
        kl = kh * jnp.exp2(bl - b)
        if ns == 1:
            bt = jnp.transpose(b[r_rows - SUBLANES:, :])
        else:
            bt = jnp.transpose(b)
        for s in range(ns):
            rs = slice(s * tt, (s + 1) * tt)
            col = SUBLANES - 1 if ns == 1 else s * tt + tt - 1
            dcol = jnp.exp2(bt[:, col:col + 1])
            st = s0_ref[s, hh] if has_state else sout_ref[0, hh]
            oi_ref[rs, vs] = _dot(qe[rs].astype(BF16), st.astype(BF16))
            vrows = vh if ns == 1 else v_ref[:, vs].astype(F32)[rs].astype(BF16)
            snew = dcol * st + _dot_tn(kl[rs].astype(BF16), vrows)
            if has_state:
                sout_ref[s, hh] = snew
            else:
                sout_ref[0, hh] = snew
        o = o_intra + oi_ref[:, vs]
        on = o * lax.rsqrt(jnp.mean(o * o, axis=-1, keepdims=True) + EPS) * gng_ref[:, vs]
        zg = zg_ref[:, vs]
        ybin_ref[:, vs] = (on * (zg * jax.nn.sigmoid(zg))).astype(BF16)
    yb = _dot(ybin_ref[...], wgo_ref[...])

    gates = jax.nn.sigmoid(gl_ref[...])
    merged = gates[:, :D_MODEL] * ya + gates[:, D_MODEL:] * yb
    out = x_ref[0] + _dot(merged.astype(BF16), wout_ref[...])
    ms2 = jnp.mean(out * out, axis=-1, keepdims=True)
    y_ref[0] = out * lax.rsqrt(ms2 + EPS) * fng_ref[...]


def _resident(shape):
    nd = len(shape)
    return pl.BlockSpec(shape, lambda *_: (0,) * nd, pipeline_mode=pl.Buffered(1))


def _run_group(x, weights, ns, tt, state=None):
    has_state = state is not None
    r_rows = ns * tt
    groups, rows_total, _ = x.shape
    nt = rows_total // r_rows
    tile = lambda c: pl.BlockSpec((1, r_rows, c), lambda i, j: (i, j, 0))
    in_specs = [tile(D_MODEL)]
    args = [x]
    if has_state:
        spad, s0 = state
        in_specs += [tile(W_LRU),
                     pl.BlockSpec((ns, GLA_HEADS, GLA_DK, GLA_DV), lambda i, j: (i, 0, 0, 0))]
        args += [spad, s0]
    in_specs += [_resident(w.shape) for w in weights]
    args += list(weights)

    tail = (1, ns * SUBLANES, W_LRU)
    s_blk = (ns, GLA_HEADS, GLA_DK, GLA_DV) if has_state else (1, GLA_HEADS, GLA_DK, GLA_DV)
    out_shape = (
        jax.ShapeDtypeStruct(x.shape, F32),
        jax.ShapeDtypeStruct((groups,) + tail[1:], F32),
        jax.ShapeDtypeStruct((groups,) + tail[1:], F32),
        jax.ShapeDtypeStruct((groups * s_blk[0],) + s_blk[1:], F32),
    )
    out_specs = (
        tile(D_MODEL),
        pl.BlockSpec(tail, lambda i, j: (i, 0, 0)),
        pl.BlockSpec(tail, lambda i, j: (i, 0, 0)),
        pl.BlockSpec(s_blk, lambda i, j: (i, 0, 0, 0)),
    )
    hk = GLA_HEADS * GLA_DK
    hv = GLA_HEADS * GLA_DV
    scratch = [
        pltpu.VMEM((r_rows, D_MODEL), BF16),
        pltpu.VMEM((LRU_BLOCKS, r_rows, LRU_BW), F32),
        pltpu.VMEM((LRU_BLOCKS, r_rows, LRU_BW), F32),
        pltpu.VMEM((r_rows, W_LRU), F32),
        pltpu.VMEM((r_rows, hk), F32),
        pltpu.VMEM((r_rows, hk), F32),
        pltpu.VMEM((r_rows, hk), F32),
        pltpu.VMEM((r_rows, hk), F32),
        pltpu.VMEM((r_rows, hv), BF16),
        pltpu.VMEM((r_rows, hv), F32),
        pltpu.VMEM((r_rows, 2 * D_MODEL), F32),
        pltpu.VMEM((r_rows, hv), F32),
        pltpu.VMEM((r_rows, hv), BF16),
        pltpu.VMEM((r_rows, r_rows), BF16),
        pltpu.VMEM((r_rows, r_rows), jnp.int32),
        pltpu.VMEM((r_rows, r_rows), F32),
    ]
    return pl.pallas_call(
        functools.partial(_mixer_kernel, ns, tt, has_state),
        grid=(groups, nt),
        in_specs=in_specs,
        out_specs=out_specs,
        out_shape=out_shape,
        scratch_shapes=scratch,
        compiler_params=pltpu.CompilerParams(
            dimension_semantics=("arbitrary", "arbitrary"),
            vmem_limit_bytes=VMEM_LIMIT),
        name="mixer_sample" if has_state else "mixer_prompt",
    )(*args)


def kernel(x_prompt, x_sample, state_lru_h, state_lru_conv, state_gla, norm_g, w_in, conv_w, conv_b,
           w_r, b_r, w_i, b_i, lam, w_a2, b_a, gla_norm_g, w_lru_o, w_gla_o, b_merge, w_out, final_norm_g):
    assert norm_g.shape[0] == 1, "single-layer trunk"
    bp, seq, _ = x_prompt.shape
    bs, dseq, _ = x_sample.shape
    assert seq % PROMPT_TILE == 0 and dseq == SUBLANES and bs % SAMPLE_SEQS == 0

    row = lambda p: p[0].reshape(1, -1).astype(F32)
    offs = [0]
    for sz in IN_SIZES:
        offs.append(offs[-1] + sz)
    win = w_in[0]
    piece = lambda i: win[:, offs[i]:offs[i + 1]].astype(BF16)
    w_al = jnp.pad(piece(6), ((0, 0), (0, LANES - GATE_RANK)))
    w_a2p = jnp.pad(w_a2[0].astype(BF16), ((0, LANES - GATE_RANK), (0, 0)))
    weights = (
        row(norm_g), piece(0), piece(1), piece(2), piece(3), piece(4), piece(5), w_al, piece(7),
        conv_w[0].astype(F32), row(conv_b),
        jnp.concatenate([w_r[0], w_i[0]], axis=-1).astype(BF16), row(b_r), row(b_i),
        row(lam), w_a2p, row(b_a), row(gla_norm_g), w_lru_o[0].astype(BF16), w_gla_o[0].astype(BF16),
        row(b_merge), w_out[0].astype(BF16), final_norm_g.reshape(1, -1).astype(F32),
    )

    yp, hp, cp, sp = _run_group(x_prompt, weights, 1, PROMPT_TILE)

    ns = SAMPLE_SEQS
    spad = jnp.concatenate(
        [state_lru_h[0][:, None, :],
         jnp.zeros((bs, SUBLANES - CONV_W, W_LRU), F32),
         state_lru_conv[0].astype(F32)], axis=1)
    ys, hs, cs, ss = _run_group(
        x_sample.reshape(bs // ns, ns * dseq, D_MODEL), weights, ns, dseq,
        state=(spad.reshape(bs // ns, ns * dseq, W_LRU), state_gla[0]))

    tailp = lambda t: t.reshape(bp, SUBLANES, W_LRU)
    tails = lambda t: t.reshape(bs, SUBLANES, W_LRU)
    keep = SUBLANES - (CONV_W - 1)
    return (yp, ys.reshape(bs, dseq, D_MODEL),
            tailp(hp)[None, :, SUBLANES - 1], tailp(cp)[None, :, keep:], sp[None],
            tails(hs)[None, :, SUBLANES - 1], tails(cs)[None, :, keep:], ss[None])
```

```python
import functools
import math

import jax
import jax.numpy as jnp
from jax import lax
from jax.experimental import pallas as pl
from jax.experimental.pallas import tpu as pltpu

D_MODEL = 1024
W_LRU = 1024
LRU_BLOCKS = 8
LRU_BW = W_LRU // LRU_BLOCKS
CONV_W = 4
RG_C = 8.0
GLA_HEADS = 4
GLA_DK = 128
GLA_DV = 256
GATE_RANK = 16
GATE_NORM = 16.0
EPS = 1e-6
IN_SIZES = (W_LRU, W_LRU, GLA_HEADS * GLA_DK, GLA_HEADS * GLA_DK,
            GLA_HEADS * GLA_DV, GLA_HEADS * GLA_DV, GATE_RANK, 2 * D_MODEL)

LANES = 128
SUBLANES = 8
STEP = 4
WINDOW = SUBLANES * STEP
PROMPT_TILE = 256
SAMPLE_SEQS = 8
VMEM_LIMIT = 56 * 1024 * 1024
LOG2E = math.log2(math.e)

F32 = jnp.float32
BF16 = jnp.bfloat16


def _dot(a, b):
    return jnp.dot(a, b, preferred_element_type=F32)


def _wdot(a, w_packed):
    return _dot(a, pltpu.bitcast(w_packed, BF16))


def _dot_nt(a, b):
    return lax.dot_general(a, b, (((1,), (1,)), ((), ())), preferred_element_type=F32)


def _dot_tn(a, b):
    return lax.dot_general(a, b, (((0,), (0,)), ((), ())), preferred_element_type=F32)


def _seg_bcast(arr, blk, idx):
    rows, cols = arr.shape
    bb = arr.reshape(rows // blk, blk, cols)
    return jnp.broadcast_to(bb[:, idx:idx + 1, :], bb.shape).reshape(rows, cols)


def _split3(x):
    hi = x.astype(BF16)
    r1 = x - hi.astype(F32)
    mid = r1.astype(BF16)
    lo = (r1 - mid.astype(F32)).astype(BF16)
    return hi, mid, lo


def _lru_gates(xc, n, wri_ref, br_ref, bi_ref, lam_ref):
    sl = slice(n * LRU_BW, (n + 1) * LRU_BW)
    pre = _wdot(xc.astype(BF16), wri_ref[n])
    rg = jax.nn.sigmoid(pre[:, :LRU_BW] + br_ref[:, sl])
    ig = jax.nn.sigmoid(pre[:, LRU_BW:] + bi_ref[:, sl])
    log_a = (-RG_C) * rg * jax.nn.softplus(-lam_ref[:, sl])
    a = jnp.exp(log_a)
    u = jnp.sqrt(-jnp.tanh(log_a) * (a * a + 1.0)) * (ig * xc)
    return a, u


def _scan_sublanes(a, u, pos, length):
    s = 1
    while s < length:
        m = pos >= s
        u_sh = jnp.where(m, pltpu.roll(u, s, 0), 0.0)
        a_sh = jnp.where(m, pltpu.roll(a, s, 0), 1.0)
        u = a * u_sh + u
        a = a * a_sh
        s *= 2
    return a, u


def _mixer_kernel(ns, tt, has_state, *refs):
    r_rows = ns * tt
    it = iter(refs)
    x_ref = next(it)
    if has_state:
        spad_ref = next(it)
        s0_ref = next(it)
    (ng_ref, wxl_ref, wzl_ref, wq_ref, wk_ref, wv_ref, wzg_ref, wal_ref, wgl_ref,
     cw_ref, cb_ref, wri_ref, br_ref, bi_ref, lam_ref, wa2_ref, ba_ref,
     gng_ref, wlo_ref, wgo_ref, bm_ref, wout_ref, fng_ref) = [next(it) for _ in range(23)]
    y_ref, htail_ref, xtail_ref, sout_ref = [next(it) for _ in range(4)]
    (xn_ref, xl_ref, h_ref, zl_ref, q_ref, k_ref, g_ref, b_ref, v_ref, zg_ref,
     gl_ref, oi_ref, ybin_ref, lseg_ref, code_ref, att_ref) = [next(it) for _ in range(16)]

    if not has_state:
        @pl.when(pl.program_id(1) == 0)
        def _():
            htail_ref[...] = jnp.zeros_like(htail_ref)
            xtail_ref[...] = jnp.zeros_like(xtail_ref)
            sout_ref[...] = jnp.zeros_like(sout_ref)

    @pl.when((pl.program_id(0) == 0) & (pl.program_id(1) == 0))
    def _():
        ti = lax.broadcasted_iota(jnp.int32, (r_rows, r_rows), 0)
        si = lax.broadcasted_iota(jnp.int32, (r_rows, r_rows), 1)
        tx = ti ^ si
        same = tx < tt
        lseg_ref[...] = jnp.where(si <= ti, jnp.where(same, 1.0, 0.0), 0.0).astype(BF16)
        code_ref[...] = jnp.where(si < ti, jnp.where(same, 32 - lax.clz(tx), 0), 0)

    x = x_ref[0]
    ms = jnp.mean(x * x, axis=-1, keepdims=True)
    xn_ref[...] = (x * lax.rsqrt(ms + EPS) * ng_ref[...]).astype(BF16)

    def proj_zl():
        zl_ref[...] = _wdot(xn_ref[...], wzl_ref[...])

    def proj_qk():
        q_ref[...] = _wdot(xn_ref[...], wq_ref[...]) * (GLA_DK ** -0.5)
        k_ref[...] = _wdot(xn_ref[...], wk_ref[...])

    def proj_v():
        v_ref[...] = _wdot(xn_ref[...], wv_ref[...]).astype(BF16)

    def proj_zg():
        zg_ref[...] = _wdot(xn_ref[...], wzg_ref[...])

    def proj_g():
        al = _wdot(xn_ref[...], wal_ref[...])
        gpre = _wdot(al.astype(BF16), wa2_ref[...]) + ba_ref[...]
        g = jax.nn.log_sigmoid(gpre) * (LOG2E / GATE_NORM)
        g_ref[...] = g
        g1, g2, g3 = _split3(g)
        lseg = lseg_ref[...]
        b_ref[...] = _dot(lseg, g1) + _dot(lseg, g2) + _dot(lseg, g3)

    def proj_gl(half):
        cs = slice(half * D_MODEL, (half + 1) * D_MODEL)
        gl_ref[:, cs] = _wdot(xn_ref[...], wgl_ref[:, cs]) + bm_ref[:, cs]

    jobs = [proj_zl, proj_qk, proj_v, proj_zg, proj_g,
            functools.partial(proj_gl, 0), functools.partial(proj_gl, 1), None]

    xl_full = _wdot(xn_ref[...], wxl_ref[...])
    for n in range(LRU_BLOCKS):
        xl_ref[n] = xl_full[:, n * LRU_BW:(n + 1) * LRU_BW]
    sub = lax.broadcasted_iota(jnp.int32, (SUBLANES, LANES), 0)
    for n in range(LRU_BLOCKS):
        sl = slice(n * LRU_BW, (n + 1) * LRU_BW)
        cw = [cw_ref[j:j + 1, sl] for j in range(CONV_W)]
        if has_state:
            rowl = lax.broadcasted_iota(jnp.int32, (r_rows, LANES), 0)
            tpos = rowl & (tt - 1)
            xl = xl_ref[n]
            prev = spad_ref[0, :, sl]
            xc = cb_ref[:, sl] + cw[CONV_W - 1] * xl
            for kk in range(1, CONV_W):
                sh = jnp.where(tpos >= kk, pltpu.roll(xl, kk, 0),
                               pltpu.roll(prev, (kk - tt) % r_rows, 0))
                xc = xc + cw[CONV_W - 1 - kk] * sh
            a, u = _lru_gates(xc, n, wri_ref, br_ref, bi_ref, lam_ref)
            u = u + jnp.where(tpos == 0, a * prev, 0.0)
            _, h = _scan_sublanes(a, u, tpos, tt)
            h_ref[n] = h
            htail_ref[0, :, sl] = h
            xtail_ref[0, :, sl] = xl
        else:
            nw = r_rows // WINDOW
            xv = [[xl_ref[n, pl.ds(w * WINDOW + r, SUBLANES, stride=STEP), :] for r in range(STEP)]
                  for w in range(nw)]
            xtail = xtail_ref[0, :, sl]
            hprev = htail_ref[0, SUBLANES - 1:SUBLANES, sl]
            rolled = [[None] + [pltpu.roll(xv[w][q], 1, 0) for q in range(1, STEP)] for w in range(nw)]

            def shifted(w, r, kk):
                if r >= kk:
                    return xv[w][r - kk]
                q = r - kk + STEP
                before = rolled[w - 1][q] if w else xtail[SUBLANES + r - kk:SUBLANES + r - kk + 1, :]
                return jnp.where(sub == 0, before, rolled[w][q])

            xc = jnp.concatenate(
                [cb_ref[:, sl] + cw[CONV_W - 1] * xv[w][r]
                 + sum(cw[CONV_W - 1 - kk] * shifted(w, r, kk) for kk in range(1, CONV_W))
                 for w in range(nw) for r in range(STEP)], axis=0)
            a, u = _lru_gates(xc, n, wri_ref, br_ref, bi_ref, lam_ref)
            cprev = jnp.zeros((SUBLANES, LANES), F32)
            for w in range(nw):
                base = w * WINDOW
                av = [a[base + r * SUBLANES:base + (r + 1) * SUBLANES] for r in range(STEP)]
                uv = [u[base + r * SUBLANES:base + (r + 1) * SUBLANES] for r in range(STEP)]
                if w == 0:
                    uv[0] = uv[0] + jnp.where(sub == 0, av[0] * hprev, 0.0)
                hl, pr = [uv[0]], [av[0]]
                for r in range(1, STEP):
                    hl.append(av[r] * hl[-1] + uv[r])
                    pr.append(av[r] * pr[-1])
                pc, hc = _scan_sublanes(pr[-1], hl[-1], sub, SUBLANES)
                cend = hc + pc * cprev
                cin = jnp.where(sub == 0, cprev, pltpu.roll(cend, 1, 0))
                cprev = jnp.broadcast_to(cend[SUBLANES - 1:, :], (SUBLANES, LANES))
                for r in range(STEP):
                    h_ref[n, pl.ds(base + r, SUBLANES, stride=STEP), :] = hl[r] + pr[r] * cin
            htail_ref[0, :, sl] = h_ref[n, r_rows - SUBLANES:, :]
            xtail_ref[0, :, sl] = xl_ref[n, r_rows - SUBLANES:, :]
        if jobs[n] is not None:
            jobs[n]()

    zl = zl_ref[...]
    hfull = jnp.concatenate([h_ref[n] for n in range(LRU_BLOCKS)], axis=1)
    ya = _wdot((hfull * (zl * jax.nn.sigmoid(zl))).astype(BF16), wlo_ref[...])

    rowl = lax.broadcasted_iota(jnp.int32, (r_rows, LANES), 0)
    dblk = min(r_rows, LANES)

    for hh in range(GLA_HEADS):
        ks = slice(hh * GLA_DK, (hh + 1) * GLA_DK)
        vs = slice(hh * GLA_DV, (hh + 1) * GLA_DV)
        qh = q_ref[:, ks]
        kh = k_ref[:, ks]
        gh = g_ref[:, ks]
        vh = v_ref[:, vs]
        b = b_ref[:, ks]
        bl = _seg_bcast(b, tt, tt - 1)

        zlow = []
        m = 1
        while m < min(tt, SUBLANES):
            if m == 1:
                dd = jnp.where((rowl & 1) == 1, gh, 0.0)
            elif m == 2:
                c4 = rowl & 3
                dd = jnp.where(c4 == 0, pltpu.roll(gh, r_rows - 1, 0),
                               jnp.where(c4 == 2, gh,
                                         jnp.where(c4 == 3, gh + pltpu.roll(gh, 1, 0), 0.0)))
            else:
                dd = -jnp.abs(b - _seg_bcast(b, 2 * m, m - 1))
            zlow.append((jnp.where((rowl & m) != 0, qh, kh) * jnp.exp2(dd)).astype(BF16))
            m *= 2
        for i in range(r_rows // dblk):
            rs = slice(i * dblk, (i + 1) * dblk)
            codeb = code_ref[rs, rs]
            attb = jnp.zeros((dblk, dblk), F32)
            for lvl, z in enumerate(zlow, start=1):
                attb = jnp.where(codeb == lvl, _dot_nt(z[rs], z[rs]), attb)
            att_ref[rs, rs] = attb
            for j in range(i + 1, r_rows // dblk):
                att_ref[rs, j * dblk:(j + 1) * dblk] = jnp.zeros((dblk, dblk), F32)
        lvl = len(zlow) + 1
        while m < tt:
            lows, ups = [], []
            for blk in range(r_rows // (2 * m)):
                r0 = blk * 2 * m
                bref = b[r0 + m - 1:r0 + m, :]
                lows.append(kh[r0:r0 + m] * jnp.exp2(bref - b[r0:r0 + m]))
                ups.append(qh[r0 + m:r0 + 2 * m] * jnp.exp2(b[r0 + m:r0 + 2 * m] - bref))
            z = jnp.concatenate([p for pair in zip(lows, ups) for p in pair], axis=0).astype(BF16)
            zup = jnp.concatenate(ups, axis=0).astype(BF16)
            prod = _dot_nt(zup, z)
            for blk in range(r_rows // (2 * m)):
                r0 = blk * 2 * m
                cv = (r0 // dblk) * dblk
                rr, cc = slice(r0 + m, r0 + 2 * m), slice(cv, cv + dblk)
                pblk = prod[blk * m:(blk + 1) * m, cc]
                if m == dblk:
                    att_ref[rr, cc] = pblk
                else:
                    att_ref[rr, cc] = jnp.where(code_ref[rr, cc] == lvl, pblk, att_ref[rr, cc])
            m *= 2
            lvl += 1
        dg = jnp.sum(qh * kh, axis=-1, keepdims=True)
        o_intra = _dot(att_ref[...].astype(BF16), vh) + dg * vh.astype(F32)

        qe = qh * jnp.exp2(b)
        kl = kh * jnp.exp2(bl - b)
        if ns == 1:
            bt = jnp.transpose(b[r_rows - SUBLANES:, :])
        else:
            bt = jnp.transpose(b)
        for s in range(ns):
            rs = slice(s * tt, (s + 1) * tt)
            col = SUBLANES - 1 if ns == 1 else s * tt + tt - 1
            dcol = jnp.exp2(bt[:, col:col + 1])
            st = s0_ref[s, hh] if has_state else sout_ref[0, hh]
            oi_ref[rs, vs] = _dot(qe[rs].astype(BF16), st.astype(BF16))
            vrows = vh if ns == 1 else v_ref[:, vs].astype(F32)[rs].astype(BF16)
            snew = dcol * st + _dot_tn(kl[rs].astype(BF16), vrows)
            if has_state:
                sout_ref[s, hh] = snew
            else:
                sout_ref[0, hh] = snew
        o = o_intra + oi_ref[:, vs]
        on = o * lax.rsqrt(jnp.mean(o * o, axis=-1, keepdims=True) + EPS) * gng_ref[:, vs]
        zg = zg_ref[:, vs]
        ybin_ref[:, vs] = (on * (zg * jax.nn.sigmoid(zg))).astype(BF16)
    yb = _wdot(ybin_ref[...], wgo_ref[...])

    gates = jax.nn.sigmoid(gl_ref[...])
    merged = gates[:, :D_MODEL] * ya + gates[:, D_MODEL:] * yb
    out = x_ref[0] + _wdot(merged.astype(BF16), wout_ref[...])
    ms2 = jnp.mean(out * out, axis=-1, keepdims=True)
    y_ref[0] = out * lax.rsqrt(ms2 + EPS) * fng_ref[...]


def _resident(shape):
    nd = len(shape)
    return pl.BlockSpec(shape, lambda *_: (0,) * nd, pipeline_mode=pl.Buffered(1))


def _run_group(x, weights, ns, tt, state=None):
    has_state = state is not None
    r_rows = ns * tt
    groups, rows_total, _ = x.shape
    nt = rows_total // r_rows
    tile = lambda c: pl.BlockSpec((1, r_rows, c), lambda i, j: (i, j, 0))
    in_specs = [tile(D_MODEL)]
    args = [x]
    if has_state:
        spad, s0 = state
        in_specs += [tile(W_LRU),
                     pl.BlockSpec((ns, GLA_HEADS, GLA_DK, GLA_DV), lambda i, j: (i, 0, 0, 0))]
        args += [spad, s0]
    in_specs += [_resident(w.shape) for w in weights]
    args += list(weights)

    tail = (1, ns * SUBLANES, W_LRU)
    s_blk = (ns, GLA_HEADS, GLA_DK, GLA_DV) if has_state else (1, GLA_HEADS, GLA_DK, GLA_DV)
    out_shape = (
        jax.ShapeDtypeStruct(x.shape, F32),
        jax.ShapeDtypeStruct((groups,) + tail[1:], F32),
        jax.ShapeDtypeStruct((groups,) + tail[1:], F32),
        jax.ShapeDtypeStruct((groups * s_blk[0],) + s_blk[1:], F32),
    )
    out_specs = (
        tile(D_MODEL),
        pl.BlockSpec(tail, lambda i, j: (i, 0, 0)),
        pl.BlockSpec(tail, lambda i, j: (i, 0, 0)),
        pl.BlockSpec(s_blk, lambda i, j: (i, 0, 0, 0)),
    )
    hk = GLA_HEADS * GLA_DK
    hv = GLA_HEADS * GLA_DV
    scratch = [
        pltpu.VMEM((r_rows, D_MODEL), BF16),
        pltpu.VMEM((LRU_BLOCKS, r_rows, LRU_BW), F32),
        pltpu.VMEM((LRU_BLOCKS, r_rows, LRU_BW), F32),
        pltpu.VMEM((r_rows, W_LRU), F32),
        pltpu.VMEM((r_rows, hk), F32),
        pltpu.VMEM((r_rows, hk), F32),
        pltpu.VMEM((r_rows, hk), F32),
        pltpu.VMEM((r_rows, hk), F32),
        pltpu.VMEM((r_rows, hv), BF16),
        pltpu.VMEM((r_rows, hv), F32),
        pltpu.VMEM((r_rows, 2 * D_MODEL), F32),
        pltpu.VMEM((r_rows, hv), F32),
        pltpu.VMEM((r_rows, hv), BF16),
        pltpu.VMEM((r_rows, r_rows), BF16),
        pltpu.VMEM((r_rows, r_rows), jnp.int32),
        pltpu.VMEM((r_rows, r_rows), F32),
    ]
    return pl.pallas_call(
        functools.partial(_mixer_kernel, ns, tt, has_state),
        grid=(groups, nt),
        in_specs=in_specs,
        out_specs=out_specs,
        out_shape=out_shape,
        scratch_shapes=scratch,
        compiler_params=pltpu.CompilerParams(
            dimension_semantics=("arbitrary", "arbitrary"),
            vmem_limit_bytes=VMEM_LIMIT),
        name="mixer_sample" if has_state else "mixer_prompt",
    )(*args)


def kernel(x_prompt, x_sample, state_lru_h, state_lru_conv, state_gla, norm_g, w_in, conv_w, conv_b,
           w_r, b_r, w_i, b_i, lam, w_a2, b_a, gla_norm_g, w_lru_o, w_gla_o, b_merge, w_out, final_norm_g):
    assert norm_g.shape[0] == 1, "single-layer trunk"
    bp, seq, _ = x_prompt.shape
    bs, dseq, _ = x_sample.shape
    assert seq % PROMPT_TILE == 0 and dseq == SUBLANES and bs % SAMPLE_SEQS == 0

    row = lambda p: p[0].reshape(1, -1).astype(F32)

    def packed(w):
        *lead, kdim, ndim = w.shape
        pairs = w.astype(BF16).reshape(*lead, kdim // 2, 2, ndim)
        return lax.bitcast_convert_type(jnp.swapaxes(pairs, -1, -2), jnp.uint32)

    offs = [0]
    for sz in IN_SIZES:
        offs.append(offs[-1] + sz)
    win = w_in[0]
    piece = lambda i: packed(win[:, offs[i]:offs[i + 1]])
    w_al = packed(jnp.pad(win[:, offs[6]:offs[7]], ((0, 0), (0, LANES - GATE_RANK))))
    w_a2p = packed(jnp.pad(w_a2[0], ((0, LANES - GATE_RANK), (0, 0))))
    weights = (
        row(norm_g), piece(0), piece(1), piece(2), piece(3), piece(4), piece(5), w_al, piece(7),
        conv_w[0].astype(F32), row(conv_b),
        packed(jnp.concatenate([w_r[0], w_i[0]], axis=-1)), row(b_r), row(b_i),
        row(lam), w_a2p, row(b_a), row(gla_norm_g), packed(w_lru_o[0]), packed(w_gla_o[0]),
        row(b_merge), packed(w_out[0]), final_norm_g.reshape(1, -1).astype(F32),
    )

    yp, hp, cp, sp = _run_group(x_prompt, weights, 1, PROMPT_TILE)

    ns = SAMPLE_SEQS
    spad = jnp.concatenate(
        [state_lru_h[0][:, None, :],
         jnp.zeros((bs, SUBLANES - CONV_W, W_LRU), F32),
         state_lru_conv[0].astype(F32)], axis=1)
    ys, hs, cs, ss = _run_group(
        x_sample.reshape(bs // ns, ns * dseq, D_MODEL), weights, ns, dseq,
        state=(spad.reshape(bs // ns, ns * dseq, W_LRU), state_gla[0]))

    tailp = lambda t: t.reshape(bp, SUBLANES, W_LRU)
    tails = lambda t: t.reshape(bs, SUBLANES, W_LRU)
    keep = SUBLANES - (CONV_W - 1)
    return (yp, ys.reshape(bs, dseq, D_MODEL),
            tailp(hp)[None, :, SUBLANES - 1], tailp(cp)[None, :, keep:], sp[None],
            tails(hs)[None, :, SUBLANES - 1], tails(cs)[None, :, keep:], ss[None])
```

```python
import functools
import math

import jax
import jax.numpy as jnp
from jax import lax
from jax.experimental import pallas as pl
from jax.experimental.pallas import tpu as pltpu

D_MODEL = 1024
W_LRU = 1024
LRU_BLOCKS = 8
LRU_BW = W_LRU // LRU_BLOCKS
CONV_W = 4
RG_C = 8.0
GLA_HEADS = 4
GLA_DK = 128
GLA_DV = 256
GATE_RANK = 16
GATE_NORM = 16.0
EPS = 1e-6
IN_SIZES = (W_LRU, W_LRU, GLA_HEADS * GLA_DK, GLA_HEADS * GLA_DK,
            GLA_HEADS * GLA_DV, GLA_HEADS * GLA_DV, GATE_RANK, 2 * D_MODEL)

LANES = 128
SUBLANES = 8
STEP = 4
WINDOW = SUBLANES * STEP
PROMPT_TILE = 256
SAMPLE_SEQS = 8
VMEM_LIMIT = 56 * 1024 * 1024
PACK_ROWS = 128
PACK_VMEM_LIMIT = 40 * 1024 * 1024
LOG2E = math.log2(math.e)

F32 = jnp.float32
BF16 = jnp.bfloat16


def _dot(a, b):
    return jnp.dot(a, b, preferred_element_type=F32)


def _wdot(a, w_packed):
    return _dot(a, pltpu.bitcast(w_packed, BF16))


def _dot_nt(a, b):
    return lax.dot_general(a, b, (((1,), (1,)), ((), ())), preferred_element_type=F32)


def _dot_tn(a, b):
    return lax.dot_general(a, b, (((0,), (0,)), ((), ())), preferred_element_type=F32)


def _seg_bcast(arr, blk, idx):
    rows, cols = arr.shape
    bb = arr.reshape(rows // blk, blk, cols)
    return jnp.broadcast_to(bb[:, idx:idx + 1, :], bb.shape).reshape(rows, cols)


def _split3(x):
    hi = x.astype(BF16)
    r1 = x - hi.astype(F32)
    mid = r1.astype(BF16)
    lo = (r1 - mid.astype(F32)).astype(BF16)
    return hi, mid, lo


def _lru_gates(xc, n, wri_ref, br_ref, bi_ref, lam_ref):
    sl = slice(n * LRU_BW, (n + 1) * LRU_BW)
    pre = _wdot(xc.astype(BF16), wri_ref[n])
    rg = jax.nn.sigmoid(pre[:, :LRU_BW] + br_ref[:, sl])
    ig = jax.nn.sigmoid(pre[:, LRU_BW:] + bi_ref[:, sl])
    log_a = (-RG_C) * rg * jax.nn.softplus(-lam_ref[:, sl])
    a = jnp.exp(log_a)
    u = jnp.sqrt(-jnp.tanh(log_a) * (a * a + 1.0)) * (ig * xc)
    return a, u


def _scan_sublanes(a, u, pos, length):
    s = 1
    while s < length:
        m = pos >= s
        u_sh = jnp.where(m, pltpu.roll(u, s, 0), 0.0)
        a_sh = jnp.where(m, pltpu.roll(a, s, 0), 1.0)
        u = a * u_sh + u
        a = a * a_sh
        s *= 2
    return a, u


def _mixer_kernel(ns, tt, has_state, *refs):
    r_rows = ns * tt
    it = iter(refs)
    x_ref = next(it)
    if has_state:
        spad_ref = next(it)
        s0_ref = next(it)
    (ng_ref, wxl_ref, wzl_ref, wq_ref, wk_ref, wv_ref, wzg_ref, wal_ref, wgl_ref,
     cw_ref, cb_ref, wri_ref, br_ref, bi_ref, lam_ref, wa2_ref, ba_ref,
     gng_ref, wlo_ref, wgo_ref, bm_ref, wout_ref, fng_ref) = [next(it) for _ in range(23)]
    y_ref, htail_ref, xtail_ref, sout_ref = [next(it) for _ in range(4)]
    (xn_ref, xl_ref, h_ref, zl_ref, q_ref, k_ref, g_ref, b_ref, v_ref, zg_ref,
     gl_ref, oi_ref, ybin_ref, lseg_ref, code_ref, att_ref) = [next(it) for _ in range(16)]

    if not has_state:
        @pl.when(pl.program_id(1) == 0)
        def _():
            htail_ref[...] = jnp.zeros_like(htail_ref)
            xtail_ref[...] = jnp.zeros_like(xtail_ref)
            sout_ref[...] = jnp.zeros_like(sout_ref)

    @pl.when((pl.program_id(0) == 0) & (pl.program_id(1) == 0))
    def _():
        ti = lax.broadcasted_iota(jnp.int32, (r_rows, r_rows), 0)
        si = lax.broadcasted_iota(jnp.int32, (r_rows, r_rows), 1)
        tx = ti ^ si
        same = tx < tt
        lseg_ref[...] = jnp.where(si <= ti, jnp.where(same, 1.0, 0.0), 0.0).astype(BF16)
        code_ref[...] = jnp.where(si < ti, jnp.where(same, 32 - lax.clz(tx), 0), 0)

    x = x_ref[0]
    ms = jnp.mean(x * x, axis=-1, keepdims=True)
    xn_ref[...] = (x * lax.rsqrt(ms + EPS) * ng_ref[...]).astype(BF16)

    def proj_zl():
        zl_ref[...] = _wdot(xn_ref[...], wzl_ref[...])

    def proj_qk():
        q_ref[...] = _wdot(xn_ref[...], wq_ref[...]) * (GLA_DK ** -0.5)
        k_ref[...] = _wdot(xn_ref[...], wk_ref[...])

    def proj_v():
        v_ref[...] = _wdot(xn_ref[...], wv_ref[...]).astype(BF16)

    def proj_zg():
        zg_ref[...] = _wdot(xn_ref[...], wzg_ref[...])

    def proj_g():
        al = _wdot(xn_ref[...], wal_ref[...])
        gpre = _wdot(al.astype(BF16), wa2_ref[...]) + ba_ref[...]
        g = jax.nn.log_sigmoid(gpre) * (LOG2E / GATE_NORM)
        g_ref[...] = g
        g1, g2, g3 = _split3(g)
        lseg = lseg_ref[...]
        b_ref[...] = _dot(lseg, g1) + _dot(lseg, g2) + _dot(lseg, g3)

    def proj_gl(half):
        cs = slice(half * D_MODEL, (half + 1) * D_MODEL)
        gl_ref[:, cs] = _wdot(xn_ref[...], wgl_ref[:, cs]) + bm_ref[:, cs]

    jobs = [proj_zl, proj_qk, proj_v, proj_zg, proj_g,
            functools.partial(proj_gl, 0), functools.partial(proj_gl, 1), None]

    xl_full = _wdot(xn_ref[...], wxl_ref[...])
    for n in range(LRU_BLOCKS):
        xl_ref[n] = xl_full[:, n * LRU_BW:(n + 1) * LRU_BW]
    sub = lax.broadcasted_iota(jnp.int32, (SUBLANES, LANES), 0)
    for n in range(LRU_BLOCKS):
        sl = slice(n * LRU_BW, (n + 1) * LRU_BW)
        cw = [cw_ref[j:j + 1, sl] for j in range(CONV_W)]
        if has_state:
            rowl = lax.broadcasted_iota(jnp.int32, (r_rows, LANES), 0)
            tpos = rowl & (tt - 1)
            xl = xl_ref[n]
            prev = spad_ref[0, :, sl]
            xc = cb_ref[:, sl] + cw[CONV_W - 1] * xl
            for kk in range(1, CONV_W):
                sh = jnp.where(tpos >= kk, pltpu.roll(xl, kk, 0),
                               pltpu.roll(prev, (kk - tt) % r_rows, 0))
                xc = xc + cw[CONV_W - 1 - kk] * sh
            a, u = _lru_gates(xc, n, wri_ref, br_ref, bi_ref, lam_ref)
            u = u + jnp.where(tpos == 0, a * prev, 0.0)
            _, h = _scan_sublanes(a, u, tpos, tt)
            h_ref[n] = h
            htail_ref[0, :, sl] = h
            xtail_ref[0, :, sl] = xl
        else:
            nw = r_rows // WINDOW
            xv = [[xl_ref[n, pl.ds(w * WINDOW + r, SUBLANES, stride=STEP), :] for r in range(STEP)]
                  for w in range(nw)]
            xtail = xtail_ref[0, :, sl]
            hprev = htail_ref[0, SUBLANES - 1:SUBLANES, sl]
            rolled = [[None] + [pltpu.roll(xv[w][q], 1, 0) for q in range(1, STEP)] for w in range(nw)]

            def shifted(w, r, kk):
                if r >= kk:
                    return xv[w][r - kk]
                q = r - kk + STEP
                before = rolled[w - 1][q] if w else xtail[SUBLANES + r - kk:SUBLANES + r - kk + 1, :]
                return jnp.where(sub == 0, before, rolled[w][q])

            xc = jnp.concatenate(
                [cb_ref[:, sl] + cw[CONV_W - 1] * xv[w][r]
                 + sum(cw[CONV_W - 1 - kk] * shifted(w, r, kk) for kk in range(1, CONV_W))
                 for w in range(nw) for r in range(STEP)], axis=0)
            a, u = _lru_gates(xc, n, wri_ref, br_ref, bi_ref, lam_ref)
            cprev = jnp.zeros((SUBLANES, LANES), F32)
            for w in range(nw):
                base = w * WINDOW
                av = [a[base + r * SUBLANES:base + (r + 1) * SUBLANES] for r in range(STEP)]
                uv = [u[base + r * SUBLANES:base + (r + 1) * SUBLANES] for r in range(STEP)]
                if w == 0:
                    uv[0] = uv[0] + jnp.where(sub == 0, av[0] * hprev, 0.0)
                hl, pr = [uv[0]], [av[0]]
                for r in range(1, STEP):
                    hl.append(av[r] * hl[-1] + uv[r])
                    pr.append(av[r] * pr[-1])
                pc, hc = _scan_sublanes(pr[-1], hl[-1], sub, SUBLANES)
                cend = hc + pc * cprev
                cin = jnp.where(sub == 0, cprev, pltpu.roll(cend, 1, 0))
                cprev = jnp.broadcast_to(cend[SUBLANES - 1:, :], (SUBLANES, LANES))
                for r in range(STEP):
                    h_ref[n, pl.ds(base + r, SUBLANES, stride=STEP), :] = hl[r] + pr[r] * cin
            htail_ref[0, :, sl] = h_ref[n, r_rows - SUBLANES:, :]
            xtail_ref[0, :, sl] = xl_ref[n, r_rows - SUBLANES:, :]
        if jobs[n] is not None:
            jobs[n]()

    zl = zl_ref[...]
    hfull = jnp.concatenate([h_ref[n] for n in range(LRU_BLOCKS)], axis=1)
    ya = _wdot((hfull * (zl * jax.nn.sigmoid(zl))).astype(BF16), wlo_ref[...])

    rowl = lax.broadcasted_iota(jnp.int32, (r_rows, LANES), 0)
    dblk = min(r_rows, LANES)

    for hh in range(GLA_HEADS):
        ks = slice(hh * GLA_DK, (hh + 1) * GLA_DK)
        vs = slice(hh * GLA_DV, (hh + 1) * GLA_DV)
        qh = q_ref[:, ks]
        kh = k_ref[:, ks]
        gh = g_ref[:, ks]
        vh = v_ref[:, vs]
        b = b_ref[:, ks]
        bl = _seg_bcast(b, tt, tt - 1)

        zlow = []
        m = 1
        while m < min(tt, SUBLANES):
            if m == 1:
                dd = jnp.where((rowl & 1) == 1, gh, 0.0)
            elif m == 2:
                c4 = rowl & 3
                dd = jnp.where(c4 == 0, pltpu.roll(gh, r_rows - 1, 0),
                               jnp.where(c4 == 2, gh,
                                         jnp.where(c4 == 3, gh + pltpu.roll(gh, 1, 0), 0.0)))
            else:
                dd = -jnp.abs(b - _seg_bcast(b, 2 * m, m - 1))
            zlow.append((jnp.where((rowl & m) != 0, qh, kh) * jnp.exp2(dd)).astype(BF16))
            m *= 2
        for i in range(r_rows // dblk):
            rs = slice(i * dblk, (i + 1) * dblk)
            codeb = code_ref[rs, rs]
            attb = jnp.zeros((dblk, dblk), F32)
            for lvl, z in enumerate(zlow, start=1):
                attb = jnp.where(codeb == lvl, _dot_nt(z[rs], z[rs]), attb)
            att_ref[rs, rs] = attb
            for j in range(i + 1, r_rows // dblk):
                att_ref[rs, j * dblk:(j + 1) * dblk] = jnp.zeros((dblk, dblk), F32)
        lvl = len(zlow) + 1
        while m < tt:
            lows, ups = [], []
            for blk in range(r_rows // (2 * m)):
                r0 = blk * 2 * m
                bref = b[r0 + m - 1:r0 + m, :]
                lows.append(kh[r0:r0 + m] * jnp.exp2(bref - b[r0:r0 + m]))
                ups.append(qh[r0 + m:r0 + 2 * m] * jnp.exp2(b[r0 + m:r0 + 2 * m] - bref))
            z = jnp.concatenate([p for pair in zip(lows, ups) for p in pair], axis=0).astype(BF16)
            zup = jnp.concatenate(ups, axis=0).astype(BF16)
            prod = _dot_nt(zup, z)
            for blk in range(r_rows // (2 * m)):
                r0 = blk * 2 * m
                cv = (r0 // dblk) * dblk
                rr, cc = slice(r0 + m, r0 + 2 * m), slice(cv, cv + dblk)
                pblk = prod[blk * m:(blk + 1) * m, cc]
                if m == dblk:
                    att_ref[rr, cc] = pblk
                else:
                    att_ref[rr, cc] = jnp.where(code_ref[rr, cc] == lvl, pblk, att_ref[rr, cc])
            m *= 2
            lvl += 1
        dg = jnp.sum(qh * kh, axis=-1, keepdims=True)
        o_intra = _dot(att_ref[...].astype(BF16), vh) + dg * vh.astype(F32)

        qe = qh * jnp.exp2(b)
        kl = kh * jnp.exp2(bl - b)
        if ns == 1:
            bt = jnp.transpose(b[r_rows - SUBLANES:, :])
        else:
            bt = jnp.transpose(b)
        for s in range(ns):
            rs = slice(s * tt, (s + 1) * tt)
            col = SUBLANES - 1 if ns == 1 else s * tt + tt - 1
            dcol = jnp.exp2(bt[:, col:col + 1])
            st = s0_ref[s, hh] if has_state else sout_ref[0, hh]
            oi_ref[rs, vs] = _dot(qe[rs].astype(BF16), st.astype(BF16))
            vrows = vh if ns == 1 else v_ref[:, vs].astype(F32)[rs].astype(BF16)
            snew = dcol * st + _dot_tn(kl[rs].astype(BF16), vrows)
            if has_state:
                sout_ref[s, hh] = snew
            else:
                sout_ref[0, hh] = snew
        o = o_intra + oi_ref[:, vs]
        on = o * lax.rsqrt(jnp.mean(o * o, axis=-1, keepdims=True) + EPS) * gng_ref[:, vs]
        zg = zg_ref[:, vs]
        ybin_ref[:, vs] = (on * (zg * jax.nn.sigmoid(zg))).astype(BF16)
    yb = _wdot(ybin_ref[...], wgo_ref[...])

    gates = jax.nn.sigmoid(gl_ref[...])
    merged = gates[:, :D_MODEL] * ya + gates[:, D_MODEL:] * yb
    out = x_ref[0] + _wdot(merged.astype(BF16), wout_ref[...])
    ms2 = jnp.mean(out * out, axis=-1, keepdims=True)
    y_ref[0] = out * lax.rsqrt(ms2 + EPS) * fng_ref[...]


def _resident(shape):
    nd = len(shape)
    return pl.BlockSpec(shape, lambda *_: (0,) * nd, pipeline_mode=pl.Buffered(1))


def _run_group(x, weights, ns, tt, state=None):
    has_state = state is not None
    r_rows = ns * tt
    groups, rows_total, _ = x.shape
    nt = rows_total // r_rows
    tile = lambda c: pl.BlockSpec((1, r_rows, c), lambda i, j: (i, j, 0))
    in_specs = [tile(D_MODEL)]
    args = [x]
    if has_state:
        spad, s0 = state
        in_specs += [tile(W_LRU),
                     pl.BlockSpec((ns, GLA_HEADS, GLA_DK, GLA_DV), lambda i, j: (i, 0, 0, 0))]
        args += [spad, s0]
    in_specs += [_resident(w.shape) for w in weights]
    args += list(weights)

    tail = (1, ns * SUBLANES, W_LRU)
    s_blk = (ns, GLA_HEADS, GLA_DK, GLA_DV) if has_state else (1, GLA_HEADS, GLA_DK, GLA_DV)
    out_shape = (
        jax.ShapeDtypeStruct(x.shape, F32),
        jax.ShapeDtypeStruct((groups,) + tail[1:], F32),
        jax.ShapeDtypeStruct((groups,) + tail[1:], F32),
        jax.ShapeDtypeStruct((groups * s_blk[0],) + s_blk[1:], F32),
    )
    out_specs = (
        tile(D_MODEL),
        pl.BlockSpec(tail, lambda i, j: (i, 0, 0)),
        pl.BlockSpec(tail, lambda i, j: (i, 0, 0)),
        pl.BlockSpec(s_blk, lambda i, j: (i, 0, 0, 0)),
    )
    hk = GLA_HEADS * GLA_DK
    hv = GLA_HEADS * GLA_DV
    scratch = [
        pltpu.VMEM((r_rows, D_MODEL), BF16),
        pltpu.VMEM((LRU_BLOCKS, r_rows, LRU_BW), F32),
        pltpu.VMEM((LRU_BLOCKS, r_rows, LRU_BW), F32),
        pltpu.VMEM((r_rows, W_LRU), F32),
        pltpu.VMEM((r_rows, hk), F32),
        pltpu.VMEM((r_rows, hk), F32),
        pltpu.VMEM((r_rows, hk), F32),
        pltpu.VMEM((r_rows, hk), F32),
        pltpu.VMEM((r_rows, hv), BF16),
        pltpu.VMEM((r_rows, hv), F32),
        pltpu.VMEM((r_rows, 2 * D_MODEL), F32),
        pltpu.VMEM((r_rows, hv), F32),
        pltpu.VMEM((r_rows, hv), BF16),
        pltpu.VMEM((r_rows, r_rows), BF16),
        pltpu.VMEM((r_rows, r_rows), jnp.int32),
        pltpu.VMEM((r_rows, r_rows), F32),
    ]
    return pl.pallas_call(
        functools.partial(_mixer_kernel, ns, tt, has_state),
        grid=(groups, nt),
        in_specs=in_specs,
        out_specs=out_specs,
        out_shape=out_shape,
        scratch_shapes=scratch,
        compiler_params=pltpu.CompilerParams(
            dimension_semantics=("arbitrary", "arbitrary"),
            vmem_limit_bytes=VMEM_LIMIT),
        name="mixer_sample" if has_state else "mixer_prompt",
    )(*args)


def _pack_kernel(win_ref, wlo_ref, wgo_ref, wout_ref, wr_ref, wi_ref, wa2_ref,
                 oxl, ozl, oq, ok, ov, ozg, oal, ogl, olo, ogo, oout, ori, oa2):
    pk = lambda v: pltpu.bitcast(v.astype(BF16), jnp.uint32)
    offs = [0]
    for sz in IN_SIZES:
        offs.append(offs[-1] + sz)
    for o_ref, i in ((oxl, 0), (ozl, 1), (oq, 2), (ok, 3), (ov, 4), (ozg, 5), (ogl, 7)):
        o_ref[...] = pk(win_ref[:, offs[i]:offs[i + 1]])
    lane = lax.broadcasted_iota(jnp.int32, (PACK_ROWS, LANES), 1)
    oal[...] = pk(jnp.where(lane < GATE_RANK, win_ref[:, offs[6]:offs[6] + LANES], 0.0))
    olo[...] = pk(wlo_ref[...])
    ogo[...] = pk(wgo_ref[...])
    oout[...] = pk(wout_ref[...])
    ori[0] = pk(jnp.concatenate([wr_ref[0], wi_ref[0]], axis=1))
    oa2[...] = pk(jnp.concatenate(
        [wa2_ref[...], jnp.zeros((LANES - GATE_RANK, wa2_ref.shape[1]), F32)], axis=0))


def _pack_weights(w_in, w_lru_o, w_gla_o, w_out, w_r, w_i, w_a2):
    steps = D_MODEL // PACK_ROWS
    assert steps == LRU_BLOCKS and LRU_BW == PACK_ROWS
    half = PACK_ROWS // 2
    n_in = w_in.shape[-1]
    rows_in = lambda n: pl.BlockSpec((None, PACK_ROWS, n), lambda i: (0, i, 0))
    rows_out = lambda n: pl.BlockSpec((half, n), lambda i: (i, 0))
    gate_in = pl.BlockSpec((None, 1, LRU_BW, LRU_BW), lambda i: (0, i, 0, 0))
    hk = GLA_HEADS * GLA_DK
    widths = [IN_SIZES[0], IN_SIZES[1], IN_SIZES[2], IN_SIZES[3], IN_SIZES[4], IN_SIZES[5],
              LANES, IN_SIZES[7], D_MODEL, D_MODEL, D_MODEL]
    u32 = lambda *shape: jax.ShapeDtypeStruct(shape, jnp.uint32)
    return pl.pallas_call(
        _pack_kernel,
        grid=(steps,),
        in_specs=[rows_in(n_in), rows_in(D_MODEL), rows_in(D_MODEL), rows_in(D_MODEL), gate_in, gate_in,
                  pl.BlockSpec((None, GATE_RANK, hk), lambda i: (0, 0, 0))],
        out_specs=[rows_out(n) for n in widths]
        + [pl.BlockSpec((1, half, 2 * LRU_BW), lambda i: (i, 0, 0)),
           pl.BlockSpec((LANES // 2, hk), lambda i: (0, 0))],
        out_shape=[u32(D_MODEL // 2, n) for n in widths]
        + [u32(LRU_BLOCKS, half, 2 * LRU_BW), u32(LANES // 2, hk)],
        compiler_params=pltpu.CompilerParams(
            dimension_semantics=("arbitrary",), vmem_limit_bytes=PACK_VMEM_LIMIT),
        name="pack_weights",
    )(w_in, w_lru_o, w_gla_o, w_out, w_r, w_i, w_a2)


def kernel(x_prompt, x_sample, state_lru_h, state_lru_conv, state_gla, norm_g, w_in, conv_w, conv_b,
           w_r, b_r, w_i, b_i, lam, w_a2, b_a, gla_norm_g, w_lru_o, w_gla_o, b_merge, w_out, final_norm_g):
    assert norm_g.shape[0] == 1, "single-layer trunk"
    bp, seq, _ = x_prompt.shape
    bs, dseq, _ = x_sample.shape
    assert seq % PROMPT_TILE == 0 and dseq == SUBLANES and bs % SAMPLE_SEQS == 0

    row = lambda p: p[0].reshape(1, -1).astype(F32)
    (wxl, wzl, wq, wk, wv, wzg, wal, wgl, wlo, wgo, wout, wri, wa2p) = _pack_weights(
        w_in, w_lru_o, w_gla_o, w_out, w_r, w_i, w_a2)
    weights = (
        row(norm_g), wxl, wzl, wq, wk, wv, wzg, wal, wgl,
        conv_w[0].astype(F32), row(conv_b), wri, row(b_r), row(b_i),
        row(lam), wa2p, row(b_a), row(gla_norm_g), wlo, wgo,
        row(b_merge), wout, final_norm_g.reshape(1, -1).astype(F32),
    )

    yp, hp, cp, sp = _run_group(x_prompt, weights, 1, PROMPT_TILE)

    ns = SAMPLE_SEQS
    spad = jnp.concatenate(
        [state_lru_h[0][:, None, :],
         jnp.zeros((bs, SUBLANES - CONV_W, W_LRU), F32),
         state_lru_conv[0].astype(F32)], axis=1)
    ys, hs, cs, ss = _run_group(
        x_sample.reshape(bs // ns, ns * dseq, D_MODEL), weights, ns, dseq,
        state=(spad.reshape(bs // ns, ns * dseq, W_LRU), state_gla[0]))

    tailp = lambda t: t.reshape(bp, SUBLANES, W_LRU)
    tails = lambda t: t.reshape(bs, SUBLANES, W_LRU)
    keep = SUBLANES - (CONV_W - 1)
    return (yp, ys.reshape(bs, dseq, D_MODEL),
            tailp(hp)[None, :, SUBLANES - 1], tailp(cp)[None, :, keep:], sp[None],
            tails(hs)[None, :, SUBLANES - 1], tails(cs)[None, :, keep:], ss[None])
```

```python
import functools
import math

import jax
import jax.numpy as jnp
from jax import lax
from jax.experimental import pallas as pl
from jax.experimental.pallas import tpu as pltpu

D_MODEL = 1024
W_LRU = 1024
LRU_BLOCKS = 8
LRU_BW = W_LRU // LRU_BLOCKS
CONV_W = 4
RG_C = 8.0
GLA_HEADS = 4
GLA_DK = 128
GLA_DV = 256
GATE_RANK = 16
GATE_NORM = 16.0
EPS = 1e-6
IN_SIZES = (W_LRU, W_LRU, GLA_HEADS * GLA_DK, GLA_HEADS * GLA_DK,
            GLA_HEADS * GLA_DV, GLA_HEADS * GLA_DV, GATE_RANK, 2 * D_MODEL)

LANES = 128
SUBLANES = 8
STEP = 4
WINDOW = SUBLANES * STEP
PROMPT_TILE = 256
SAMPLE_SEQS = 8
VMEM_LIMIT = 56 * 1024 * 1024
PACK_ROWS = 128
PACK_VMEM_LIMIT = 40 * 1024 * 1024
LOG2E = math.log2(math.e)

F32 = jnp.float32
BF16 = jnp.bfloat16


def _dot(a, b):
    return jnp.dot(a, b, preferred_element_type=F32)


def _wdot(a, w_packed):
    return _dot(a, pltpu.bitcast(w_packed, BF16))


def _dot_nt(a, b):
    return lax.dot_general(a, b, (((1,), (1,)), ((), ())), preferred_element_type=F32)


def _dot_tn(a, b):
    return lax.dot_general(a, b, (((0,), (0,)), ((), ())), preferred_element_type=F32)


def _seg_bcast(arr, blk, idx):
    rows, cols = arr.shape
    bb = arr.reshape(rows // blk, blk, cols)
    return jnp.broadcast_to(bb[:, idx:idx + 1, :], bb.shape).reshape(rows, cols)


def _split3(x):
    hi = x.astype(BF16)
    r1 = x - hi.astype(F32)
    mid = r1.astype(BF16)
    lo = (r1 - mid.astype(F32)).astype(BF16)
    return hi, mid, lo


def _lru_gates(xc, n, wri_ref, br_ref, bi_ref, lam_ref):
    sl = slice(n * LRU_BW, (n + 1) * LRU_BW)
    pre = _wdot(xc.astype(BF16), wri_ref[n])
    rg = jax.nn.sigmoid(pre[:, :LRU_BW] + br_ref[:, sl])
    ig = jax.nn.sigmoid(pre[:, LRU_BW:] + bi_ref[:, sl])
    log_a = (-RG_C) * rg * jax.nn.softplus(-lam_ref[:, sl])
    a = jnp.exp(log_a)
    u = jnp.sqrt(-jnp.tanh(log_a) * (a * a + 1.0)) * (ig * xc)
    return a, u


def _scan_sublanes(a, u, pos, length):
    s = 1
    while s < length:
        m = pos >= s
        u_sh = jnp.where(m, pltpu.roll(u, s, 0), 0.0)
        a_sh = jnp.where(m, pltpu.roll(a, s, 0), 1.0)
        u = a * u_sh + u
        a = a * a_sh
        s *= 2
    return a, u


def _mixer_kernel(ns, tt, has_state, *refs):
    r_rows = ns * tt
    it = iter(refs)
    x_ref = next(it)
    if has_state:
        spad_ref = next(it)
        s0_ref = next(it)
    (ng_ref, wxl_ref, wzl_ref, wq_ref, wk_ref, wv_ref, wzg_ref, wal_ref, wgl_ref,
     cw_ref, cb_ref, wri_ref, br_ref, bi_ref, lam_ref, wa2_ref, ba_ref,
     gng_ref, wlo_ref, wgo_ref, bm_ref, wout_ref, fng_ref) = [next(it) for _ in range(23)]
    y_ref, htail_ref, xtail_ref, sout_ref = [next(it) for _ in range(4)]
    (xn_ref, xl_ref, h_ref, zl_ref, q_ref, k_ref, g_ref, b_ref, v_ref, zg_ref, gl_ref, oi_ref,
     lseg_ref, code_ref, att_ref, zlow_ref, zhi_ref, zup_ref, qe_ref, kl_ref,
     ybin_ref) = [next(it) for _ in range(21)]

    if not has_state:
        @pl.when(pl.program_id(1) == 0)
        def _():
            htail_ref[...] = jnp.zeros_like(htail_ref)
            xtail_ref[...] = jnp.zeros_like(xtail_ref)
            sout_ref[...] = jnp.zeros_like(sout_ref)

    @pl.when((pl.program_id(0) == 0) & (pl.program_id(1) == 0))
    def _():
        ti = lax.broadcasted_iota(jnp.int32, (r_rows, r_rows), 0)
        si = lax.broadcasted_iota(jnp.int32, (r_rows, r_rows), 1)
        tx = ti ^ si
        same = tx < tt
        lseg_ref[...] = jnp.where(si <= ti, jnp.where(same, 1.0, 0.0), 0.0).astype(BF16)
        code_ref[...] = jnp.where(si < ti, jnp.where(same, 32 - lax.clz(tx), 0), 0)

    x = x_ref[0]
    ms = jnp.mean(x * x, axis=-1, keepdims=True)
    xn_ref[...] = (x * lax.rsqrt(ms + EPS) * ng_ref[...]).astype(BF16)

    def proj_zl():
        zl = _wdot(xn_ref[...], wzl_ref[...])
        zl_ref[...] = zl * jax.nn.sigmoid(zl)

    def proj_qk():
        q_ref[...] = _wdot(xn_ref[...], wq_ref[...]) * (GLA_DK ** -0.5)
        k_ref[...] = _wdot(xn_ref[...], wk_ref[...])

    def proj_v():
        v_ref[...] = _wdot(xn_ref[...], wv_ref[...]).astype(BF16)

    def proj_zg():
        zg = _wdot(xn_ref[...], wzg_ref[...])
        zg_ref[...] = zg * jax.nn.sigmoid(zg)

    def proj_g():
        al = _wdot(xn_ref[...], wal_ref[...])
        yield
        gpre = _wdot(al.astype(BF16), wa2_ref[...]) + ba_ref[...]
        g = jax.nn.log_sigmoid(gpre) * (LOG2E / GATE_NORM)
        g_ref[...] = g
        g1, g2, g3 = _split3(g)
        yield
        lseg = lseg_ref[...]
        b_ref[...] = _dot(lseg, g1) + _dot(lseg, g2) + _dot(lseg, g3)

    def proj_gl(half):
        cs = slice(half * D_MODEL, (half + 1) * D_MODEL)
        gl_ref[:, cs] = jax.nn.sigmoid(_wdot(xn_ref[...], wgl_ref[:, cs]) + bm_ref[:, cs])

    rowl = lax.broadcasted_iota(jnp.int32, (r_rows, LANES), 0)
    n_low = min(tt, SUBLANES).bit_length() - 1

    def gla_prep(hh):
        ks = slice(hh * GLA_DK, (hh + 1) * GLA_DK)
        qh, kh, gh, b = q_ref[:, ks], k_ref[:, ks], g_ref[:, ks], b_ref[:, ks]
        m = 1
        for lv in range(n_low):
            if m == 1:
                dd = jnp.where((rowl & 1) == 1, gh, 0.0)
            elif m == 2:
                c4 = rowl & 3
                dd = jnp.where(c4 == 0, pltpu.roll(gh, r_rows - 1, 0),
                               jnp.where(c4 == 2, gh,
                                         jnp.where(c4 == 3, gh + pltpu.roll(gh, 1, 0), 0.0)))
            else:
                dd = -jnp.abs(b - _seg_bcast(b, 2 * m, m - 1))
            zlow_ref[lv, :, ks] = (jnp.where((rowl & m) != 0, qh, kh) * jnp.exp2(dd)).astype(BF16)
            m *= 2
        lv = 0
        while m < tt:
            lows, ups = [], []
            for blk in range(r_rows // (2 * m)):
                r0 = blk * 2 * m
                bref = b[r0 + m - 1:r0 + m, :]
                lows.append(kh[r0:r0 + m] * jnp.exp2(bref - b[r0:r0 + m]))
                ups.append(qh[r0 + m:r0 + 2 * m] * jnp.exp2(b[r0 + m:r0 + 2 * m] - bref))
            zhi_ref[lv, :, ks] = jnp.concatenate(
                [p for pair in zip(lows, ups) for p in pair], axis=0).astype(BF16)
            zup_ref[lv, :, ks] = jnp.concatenate(ups, axis=0).astype(BF16)
            m *= 2
            lv += 1
        bl = _seg_bcast(b, tt, tt - 1)
        qe_ref[:, ks] = (qh * jnp.exp2(b)).astype(qe_ref.dtype)
        kl_ref[:, ks] = (kh * jnp.exp2(bl - b)).astype(kl_ref.dtype)

    def both(*fs):
        return lambda: [f() for f in fs]

    def proj_xl():
        xl_full = _wdot(xn_ref[...], wxl_ref[...])
        for n in range(LRU_BLOCKS):
            xl_ref[n] = xl_full[:, n * LRU_BW:(n + 1) * LRU_BW]

    sub = lax.broadcasted_iota(jnp.int32, (SUBLANES, LANES), 0)

    def lru_block(n):
        sl = slice(n * LRU_BW, (n + 1) * LRU_BW)
        cw = [cw_ref[j:j + 1, sl] for j in range(CONV_W)]
        if has_state:
            rowl = lax.broadcasted_iota(jnp.int32, (r_rows, LANES), 0)
            tpos = rowl & (tt - 1)
            xl = xl_ref[n]
            prev = spad_ref[0, :, sl]
            xc = cb_ref[:, sl] + cw[CONV_W - 1] * xl
            for kk in range(1, CONV_W):
                sh = jnp.where(tpos >= kk, pltpu.roll(xl, kk, 0),
                               pltpu.roll(prev, (kk - tt) % r_rows, 0))
                xc = xc + cw[CONV_W - 1 - kk] * sh
            yield
            a, u = _lru_gates(xc, n, wri_ref, br_ref, bi_ref, lam_ref)
            u = u + jnp.where(tpos == 0, a * prev, 0.0)
            _, h = _scan_sublanes(a, u, tpos, tt)
            h_ref[n] = h
            htail_ref[0, :, sl] = h
            xtail_ref[0, :, sl] = xl
        else:
            nw = r_rows // WINDOW
            xv = [[xl_ref[n, pl.ds(w * WINDOW + r, SUBLANES, stride=STEP), :] for r in range(STEP)]
                  for w in range(nw)]
            xtail = xtail_ref[0, :, sl]
            hprev = htail_ref[0, SUBLANES - 1:SUBLANES, sl]
            rolled = [[None] + [pltpu.roll(xv[w][q], 1, 0) for q in range(1, STEP)] for w in range(nw)]

            def shifted(w, r, kk):
                if r >= kk:
                    return xv[w][r - kk]
                q = r - kk + STEP
                before = rolled[w - 1][q] if w else xtail[SUBLANES + r - kk:SUBLANES + r - kk + 1, :]
                return jnp.where(sub == 0, before, rolled[w][q])

            xc = jnp.concatenate(
                [cb_ref[:, sl] + cw[CONV_W - 1] * xv[w][r]
                 + sum(cw[CONV_W - 1 - kk] * shifted(w, r, kk) for kk in range(1, CONV_W))
                 for w in range(nw) for r in range(STEP)], axis=0)
            yield
            a, u = _lru_gates(xc, n, wri_ref, br_ref, bi_ref, lam_ref)
            cprev = jnp.zeros((SUBLANES, LANES), F32)
            for w in range(nw):
                base = w * WINDOW
                av = [a[base + r * SUBLANES:base + (r + 1) * SUBLANES] for r in range(STEP)]
                uv = [u[base + r * SUBLANES:base + (r + 1) * SUBLANES] for r in range(STEP)]
                if w == 0:
                    uv[0] = uv[0] + jnp.where(sub == 0, av[0] * hprev, 0.0)
                hl, pr = [uv[0]], [av[0]]
                for r in range(1, STEP):
                    hl.append(av[r] * hl[-1] + uv[r])
                    pr.append(av[r] * pr[-1])
                pc, hc = _scan_sublanes(pr[-1], hl[-1], sub, SUBLANES)
                cend = hc + pc * cprev
                cin = jnp.where(sub == 0, cprev, pltpu.roll(cend, 1, 0))
                cprev = jnp.broadcast_to(cend[SUBLANES - 1:, :], (SUBLANES, LANES))
                for r in range(STEP):
                    h_ref[n, pl.ds(base + r, SUBLANES, stride=STEP), :] = hl[r] + pr[r] * cin
            htail_ref[0, :, sl] = h_ref[n, r_rows - SUBLANES:, :]
            xtail_ref[0, :, sl] = xl_ref[n, r_rows - SUBLANES:, :]

    def branch_a_out():
        hfull = jnp.concatenate([h_ref[n] for n in range(LRU_BLOCKS)], axis=1)
        ya = _wdot((hfull * zl_ref[...]).astype(BF16), wlo_ref[...])
        gl_ref[:, :D_MODEL] = gl_ref[:, :D_MODEL] * ya

    dblk = min(r_rows, LANES)
    def gla_head(hh):
        ks = slice(hh * GLA_DK, (hh + 1) * GLA_DK)
        vs = slice(hh * GLA_DV, (hh + 1) * GLA_DV)
        vh = v_ref[:, vs]
        for i in range(r_rows // dblk):
            rs = slice(i * dblk, (i + 1) * dblk)
            codeb = code_ref[rs, rs]
            attb = jnp.zeros((dblk, dblk), F32)
            for lv in range(n_low):
                z = zlow_ref[lv, rs, ks]
                attb = jnp.where(codeb == lv + 1, _dot_nt(z, z), attb)
            att_ref[hh, rs, rs] = attb
            for j in range(i + 1, r_rows // dblk):
                att_ref[hh, rs, j * dblk:(j + 1) * dblk] = jnp.zeros((dblk, dblk), F32)
        m = SUBLANES
        lv = 0
        while m < tt:
            prod = _dot_nt(zup_ref[lv, :, ks], zhi_ref[lv, :, ks])
            for blk in range(r_rows // (2 * m)):
                r0 = blk * 2 * m
                cv = (r0 // dblk) * dblk
                rr, cc = slice(r0 + m, r0 + 2 * m), slice(cv, cv + dblk)
                pblk = prod[blk * m:(blk + 1) * m, cc]
                if m == dblk:
                    att_ref[hh, rr, cc] = pblk
                else:
                    att_ref[hh, rr, cc] = jnp.where(code_ref[rr, cc] == n_low + lv + 1, pblk, att_ref[hh, rr, cc])
            m *= 2
            lv += 1
        yield
        dg = jnp.sum(q_ref[:, ks] * k_ref[:, ks], axis=-1, keepdims=True)
        o_intra = _dot(att_ref[hh].astype(BF16), vh) + dg * vh.astype(F32)

        b = b_ref[:, ks]
        if ns == 1:
            bt = jnp.transpose(b[r_rows - SUBLANES:, :])
        else:
            bt = jnp.transpose(b)
        for s in range(ns):
            rs = slice(s * tt, (s + 1) * tt)
            col = SUBLANES - 1 if ns == 1 else s * tt + tt - 1
            dcol = jnp.exp2(bt[:, col:col + 1])
            st = s0_ref[s, hh] if has_state else sout_ref[0, hh]
            oi_ref[rs, vs] = _dot(qe_ref[rs, ks].astype(BF16), st.astype(BF16))
            vrows = vh if ns == 1 else v_ref[:, vs].astype(F32)[rs].astype(BF16)
            snew = dcol * st + _dot_tn(kl_ref[rs, ks].astype(BF16), vrows)
            if has_state:
                sout_ref[s, hh] = snew
            else:
                sout_ref[0, hh] = snew
        o = o_intra + oi_ref[:, vs]
        on = o * lax.rsqrt(jnp.mean(o * o, axis=-1, keepdims=True) + EPS) * gng_ref[:, vs]
        ybin_ref[:, vs] = (on * zg_ref[:, vs]).astype(BF16)

    paused = {}

    def start(fn, *args):
        paused[(fn,) + args] = gen = fn(*args)
        next(gen)

    def finish(fn, *args):
        for _ in paused.pop((fn,) + args):
            pass

    lru, head = lru_block, gla_head
    start(proj_g)
    proj_xl()
    next(paused[(proj_g,)])
    proj_qk()
    start(lru, 0), start(lru, 1)
    finish(lru, 0)
    finish(proj_g)
    proj_v()
    finish(lru, 1)
    gla_prep(0)
    start(lru, 2), start(lru, 3)
    proj_zl()
    finish(lru, 2)
    start(head, 0)
    proj_zg()
    finish(lru, 3)
    gla_prep(1)
    start(lru, 4), start(lru, 5)
    finish(head, 0)
    proj_gl(0)
    finish(lru, 4)
    start(head, 1)
    gla_prep(2)
    proj_gl(1)
    finish(lru, 5)
    finish(head, 1)
    start(lru, 6), start(lru, 7)
    start(head, 2)
    finish(lru, 6)
    gla_prep(3)
    finish(lru, 7)
    start(head, 3)
    finish(head, 2)
    branch_a_out()
    finish(head, 3)
    yb = _wdot(ybin_ref[...], wgo_ref[...])

    merged = gl_ref[:, :D_MODEL] + gl_ref[:, D_MODEL:] * yb
    out = x_ref[0] + _wdot(merged.astype(BF16), wout_ref[...])
    ms2 = jnp.mean(out * out, axis=-1, keepdims=True)
    y_ref[0] = out * lax.rsqrt(ms2 + EPS) * fng_ref[...]


def _resident(shape):
    nd = len(shape)
    return pl.BlockSpec(shape, lambda *_: (0,) * nd, pipeline_mode=pl.Buffered(1))


def _run_group(x, weights, ns, tt, state=None):
    has_state = state is not None
    r_rows = ns * tt
    groups, rows_total, _ = x.shape
    nt = rows_total // r_rows
    tile = lambda c: pl.BlockSpec((1, r_rows, c), lambda i, j: (i, j, 0))
    in_specs = [tile(D_MODEL)]
    args = [x]
    if has_state:
        spad, s0 = state
        in_specs += [tile(W_LRU),
                     pl.BlockSpec((ns, GLA_HEADS, GLA_DK, GLA_DV), lambda i, j: (i, 0, 0, 0))]
        args += [spad, s0]
    in_specs += [_resident(w.shape) for w in weights]
    args += list(weights)

    tail = (1, ns * SUBLANES, W_LRU)
    s_blk = (ns, GLA_HEADS, GLA_DK, GLA_DV) if has_state else (1, GLA_HEADS, GLA_DK, GLA_DV)
    out_shape = (
        jax.ShapeDtypeStruct(x.shape, F32),
        jax.ShapeDtypeStruct((groups,) + tail[1:], F32),
        jax.ShapeDtypeStruct((groups,) + tail[1:], F32),
        jax.ShapeDtypeStruct((groups * s_blk[0],) + s_blk[1:], F32),
    )
    out_specs = (
        tile(D_MODEL),
        pl.BlockSpec(tail, lambda i, j: (i, 0, 0)),
        pl.BlockSpec(tail, lambda i, j: (i, 0, 0)),
        pl.BlockSpec(s_blk, lambda i, j: (i, 0, 0, 0)),
    )
    hk = GLA_HEADS * GLA_DK
    hv = GLA_HEADS * GLA_DV
    n_low = min(tt, SUBLANES).bit_length() - 1
    n_high = tt.bit_length() - 1 - n_low
    state_operand = BF16 if ns == 1 else F32
    scratch = [
        pltpu.VMEM((r_rows, D_MODEL), BF16),
        pltpu.VMEM((LRU_BLOCKS, r_rows, LRU_BW), F32),
        pltpu.VMEM((LRU_BLOCKS, r_rows, LRU_BW), F32),
        pltpu.VMEM((r_rows, W_LRU), F32),
        pltpu.VMEM((r_rows, hk), F32),
        pltpu.VMEM((r_rows, hk), F32),
        pltpu.VMEM((r_rows, hk), F32),
        pltpu.VMEM((r_rows, hk), F32),
        pltpu.VMEM((r_rows, hv), BF16),
        pltpu.VMEM((r_rows, hv), F32),
        pltpu.VMEM((r_rows, 2 * D_MODEL), F32),
        pltpu.VMEM((r_rows, hv), F32),
        pltpu.VMEM((r_rows, r_rows), BF16),
        pltpu.VMEM((r_rows, r_rows), jnp.int32),
        pltpu.VMEM((GLA_HEADS, r_rows, r_rows), F32),
        pltpu.VMEM((n_low, r_rows, hk), BF16),
        pltpu.VMEM((max(n_high, 1), r_rows, hk), BF16),
        pltpu.VMEM((max(n_high, 1), r_rows // 2, hk), BF16),
        pltpu.VMEM((r_rows, hk), state_operand),
        pltpu.VMEM((r_rows, hk), state_operand),
        pltpu.VMEM((r_rows, hv), BF16),
    ]
    return pl.pallas_call(
        functools.partial(_mixer_kernel, ns, tt, has_state),
        grid=(groups, nt),
        in_specs=in_specs,
        out_specs=out_specs,
        out_shape=out_shape,
        scratch_shapes=scratch,
        compiler_params=pltpu.CompilerParams(
            dimension_semantics=("arbitrary", "arbitrary"),
            vmem_limit_bytes=VMEM_LIMIT),
        name="mixer_sample" if has_state else "mixer_prompt",
    )(*args)


def _pack_kernel(win_ref, wlo_ref, wgo_ref, wout_ref, wr_ref, wi_ref, wa2_ref,
                 oxl, ozl, oq, ok, ov, ozg, oal, ogl, olo, ogo, oout, ori, oa2):
    pk = lambda v: pltpu.bitcast(v.astype(BF16), jnp.uint32)
    offs = [0]
    for sz in IN_SIZES:
        offs.append(offs[-1] + sz)
    for o_ref, i in ((oxl, 0), (ozl, 1), (oq, 2), (ok, 3), (ov, 4), (ozg, 5), (ogl, 7)):
        o_ref[...] = pk(win_ref[:, offs[i]:offs[i + 1]])
    lane = lax.broadcasted_iota(jnp.int32, (PACK_ROWS, LANES), 1)
    oal[...] = pk(jnp.where(lane < GATE_RANK, win_ref[:, offs[6]:offs[6] + LANES], 0.0))
    olo[...] = pk(wlo_ref[...])
    ogo[...] = pk(wgo_ref[...])
    oout[...] = pk(wout_ref[...])
    ori[0] = pk(jnp.concatenate([wr_ref[0], wi_ref[0]], axis=1))
    oa2[...] = pk(jnp.concatenate(
        [wa2_ref[...], jnp.zeros((LANES - GATE_RANK, wa2_ref.shape[1]), F32)], axis=0))


def _pack_weights(w_in, w_lru_o, w_gla_o, w_out, w_r, w_i, w_a2):
    steps = D_MODEL // PACK_ROWS
    assert steps == LRU_BLOCKS and LRU_BW == PACK_ROWS
    half = PACK_ROWS // 2
    n_in = w_in.shape[-1]
    rows_in = lambda n: pl.BlockSpec((None, PACK_ROWS, n), lambda i: (0, i, 0))
    rows_out = lambda n: pl.BlockSpec((half, n), lambda i: (i, 0))
    gate_in = pl.BlockSpec((None, 1, LRU_BW, LRU_BW), lambda i: (0, i, 0, 0))
    hk = GLA_HEADS * GLA_DK
    widths = [IN_SIZES[0], IN_SIZES[1], IN_SIZES[2], IN_SIZES[3], IN_SIZES[4], IN_SIZES[5],
              LANES, IN_SIZES[7], D_MODEL, D_MODEL, D_MODEL]
    u32 = lambda *shape: jax.ShapeDtypeStruct(shape, jnp.uint32)
    return pl.pallas_call(
        _pack_kernel,
        grid=(steps,),
        in_specs=[rows_in(n_in), rows_in(D_MODEL), rows_in(D_MODEL), rows_in(D_MODEL), gate_in, gate_in,
                  pl.BlockSpec((None, GATE_RANK, hk), lambda i: (0, 0, 0))],
        out_specs=[rows_out(n) for n in widths]
        + [pl.BlockSpec((1, half, 2 * LRU_BW), lambda i: (i, 0, 0)),
           pl.BlockSpec((LANES // 2, hk), lambda i: (0, 0))],
        out_shape=[u32(D_MODEL // 2, n) for n in widths]
        + [u32(LRU_BLOCKS, half, 2 * LRU_BW), u32(LANES // 2, hk)],
        compiler_params=pltpu.CompilerParams(
            dimension_semantics=("arbitrary",), vmem_limit_bytes=PACK_VMEM_LIMIT),
        name="pack_weights",
    )(w_in, w_lru_o, w_gla_o, w_out, w_r, w_i, w_a2)


def kernel(x_prompt, x_sample, state_lru_h, state_lru_conv, state_gla, norm_g, w_in, conv_w, conv_b,
           w_r, b_r, w_i, b_i, lam, w_a2, b_a, gla_norm_g, w_lru_o, w_gla_o, b_merge, w_out, final_norm_g):
    assert norm_g.shape[0] == 1, "single-layer trunk"
    bp, seq, _ = x_prompt.shape
    bs, dseq, _ = x_sample.shape
    assert seq % PROMPT_TILE == 0 and dseq == SUBLANES and bs % SAMPLE_SEQS == 0

    row = lambda p: p[0].reshape(1, -1).astype(F32)
    (wxl, wzl, wq, wk, wv, wzg, wal, wgl, wlo, wgo, wout, wri, wa2p) = _pack_weights(
        w_in, w_lru_o, w_gla_o, w_out, w_r, w_i, w_a2)
    weights = (
        row(norm_g), wxl, wzl, wq, wk, wv, wzg, wal, wgl,
        conv_w[0].astype(F32), row(conv_b), wri, row(b_r), row(b_i),
        row(lam), wa2p, row(b_a), row(gla_norm_g), wlo, wgo,
        row(b_merge), wout, final_norm_g.reshape(1, -1).astype(F32),
    )

    yp, hp, cp, sp = _run_group(x_prompt, weights, 1, PROMPT_TILE)

    ns = SAMPLE_SEQS
    spad = jnp.concatenate(
        [state_lru_h[0][:, None, :],
         jnp.zeros((bs, SUBLANES - CONV_W, W_LRU), F32),
         state_lru_conv[0].astype(F32)], axis=1)
    ys, hs, cs, ss = _run_group(
        x_sample.reshape(bs // ns, ns * dseq, D_MODEL), weights, ns, dseq,
        state=(spad.reshape(bs // ns, ns * dseq, W_LRU), state_gla[0]))

    tailp = lambda t: t.reshape(bp, SUBLANES, W_LRU)
    tails = lambda t: t.reshape(bs, SUBLANES, W_LRU)
    keep = SUBLANES - (CONV_W - 1)
    return (yp, ys.reshape(bs, dseq, D_MODEL),
            tailp(hp)[None, :, SUBLANES - 1], tailp(cp)[None, :, keep:], sp[None],
            tails(hs)[None, :, SUBLANES - 1], tails(cs)[None, :, keep:], ss[None])
```

```python
import functools
import math

import jax
import jax.numpy as jnp
from jax import lax
from jax.experimental import pallas as pl
from jax.experimental.pallas import tpu as pltpu

D_MODEL = 1024
W_LRU = 1024
LRU_BLOCKS = 8
LRU_BW = W_LRU // LRU_BLOCKS
CONV_W = 4
RG_C = 8.0
GLA_HEADS = 4
GLA_DK = 128
GLA_DV = 256
GATE_RANK = 16
GATE_NORM = 16.0
EPS = 1e-6
IN_SIZES = (W_LRU, W_LRU, GLA_HEADS * GLA_DK, GLA_HEADS * GLA_DK,
            GLA_HEADS * GLA_DV, GLA_HEADS * GLA_DV, GATE_RANK, 2 * D_MODEL)

LANES = 128
SUBLANES = 8
STEP = 4
WINDOW = SUBLANES * STEP
PROMPT_TILE = 256
PROMPT_TILES_PER_STEP = 4
N_OUT = 4
N_SCRATCH = 21
SAMPLE_SEQS = 8
VMEM_LIMIT = 56 * 1024 * 1024
PACK_ROWS = 128
PACK_VMEM_LIMIT = 40 * 1024 * 1024
LOG2E = math.log2(math.e)

F32 = jnp.float32
BF16 = jnp.bfloat16


def _dot(a, b):
    return jnp.dot(a, b, preferred_element_type=F32)


def _wdot(a, w_packed):
    return _dot(a, pltpu.bitcast(w_packed, BF16))


def _dot_nt(a, b):
    return lax.dot_general(a, b, (((1,), (1,)), ((), ())), preferred_element_type=F32)


def _dot_tn(a, b):
    return lax.dot_general(a, b, (((0,), (0,)), ((), ())), preferred_element_type=F32)


def _seg_bcast(arr, blk, idx):
    rows, cols = arr.shape
    bb = arr.reshape(rows // blk, blk, cols)
    return jnp.broadcast_to(bb[:, idx:idx + 1, :], bb.shape).reshape(rows, cols)


def _split3(x):
    hi = x.astype(BF16)
    r1 = x - hi.astype(F32)
    mid = r1.astype(BF16)
    lo = (r1 - mid.astype(F32)).astype(BF16)
    return hi, mid, lo


def _lru_gates(xc, n, wri_ref, br_ref, bi_ref, lam_ref):
    sl = slice(n * LRU_BW, (n + 1) * LRU_BW)
    pre = _wdot(xc.astype(BF16), wri_ref[n])
    rg = jax.nn.sigmoid(pre[:, :LRU_BW] + br_ref[:, sl])
    ig = jax.nn.sigmoid(pre[:, LRU_BW:] + bi_ref[:, sl])
    log_a = (-RG_C) * rg * jax.nn.softplus(-lam_ref[:, sl])
    a = jnp.exp(log_a)
    u = jnp.sqrt(-jnp.tanh(log_a) * (a * a + 1.0)) * (ig * xc)
    return a, u


def _scan_sublanes(a, u, pos, length):
    s = 1
    while s < length:
        m = pos >= s
        u_sh = jnp.where(m, pltpu.roll(u, s, 0), 0.0)
        a_sh = jnp.where(m, pltpu.roll(a, s, 0), 1.0)
        u = a * u_sh + u
        a = a * a_sh
        s *= 2
    return a, u


def _mixer_kernel(ns, tt, nk, has_state, *refs):
    r_rows = ns * tt
    n_in = len(refs) - N_OUT - N_SCRATCH
    step_start = pl.program_id(1) == 0
    call_start = (pl.program_id(0) == 0) & step_start
    if nk == 1:
        _mixer_tile(ns, tt, has_state, step_start, call_start, *refs)
        return
    assert not has_state

    def tile(k, carry):
        rows = pl.ds(pl.multiple_of(k * r_rows, r_rows), r_rows)
        sub = list(refs)
        sub[0] = refs[0].at[:, rows, :]
        sub[n_in] = refs[n_in].at[:, rows, :]
        _mixer_tile(ns, tt, has_state, step_start & (k == 0), call_start & (k == 0), *sub)
        return carry

    lax.fori_loop(0, nk, tile, 0)


def _mixer_tile(ns, tt, has_state, seq_start, call_start, *refs):
    r_rows = ns * tt
    it = iter(refs)
    x_ref = next(it)
    if has_state:
        spad_ref = next(it)
        s0_ref = next(it)
    (ng_ref, wxl_ref, wzl_ref, wq_ref, wk_ref, wv_ref, wzg_ref, wal_ref, wgl_ref,
     cw_ref, cb_ref, wri_ref, br_ref, bi_ref, lam_ref, wa2_ref, ba_ref,
     gng_ref, wlo_ref, wgo_ref, bm_ref, wout_ref, fng_ref) = [next(it) for _ in range(23)]
    y_ref, htail_ref, xtail_ref, sout_ref = [next(it) for _ in range(4)]
    (xn_ref, xl_ref, h_ref, zl_ref, q_ref, k_ref, g_ref, b_ref, v_ref, zg_ref, gl_ref, oi_ref,
     lseg_ref, code_ref, att_ref, zlow_ref, zhi_ref, zup_ref, qe_ref, kl_ref,
     ybin_ref) = [next(it) for _ in range(21)]

    if not has_state:
        @pl.when(seq_start)
        def _():
            htail_ref[...] = jnp.zeros_like(htail_ref)
            xtail_ref[...] = jnp.zeros_like(xtail_ref)
            sout_ref[...] = jnp.zeros_like(sout_ref)

    @pl.when(call_start)
    def _():
        ti = lax.broadcasted_iota(jnp.int32, (r_rows, r_rows), 0)
        si = lax.broadcasted_iota(jnp.int32, (r_rows, r_rows), 1)
        tx = ti ^ si
        same = tx < tt
        lseg_ref[...] = jnp.where(si <= ti, jnp.where(same, 1.0, 0.0), 0.0).astype(BF16)
        code_ref[...] = jnp.where(si < ti, jnp.where(same, 32 - lax.clz(tx), 0), 0)

    x = x_ref[0]
    ms = jnp.mean(x * x, axis=-1, keepdims=True)
    xn_ref[...] = (x * lax.rsqrt(ms + EPS) * ng_ref[...]).astype(BF16)

    def proj_zl():
        zl = _wdot(xn_ref[...], wzl_ref[...])
        zl_ref[...] = zl * jax.nn.sigmoid(zl)

    def proj_qk():
        q_ref[...] = _wdot(xn_ref[...], wq_ref[...]) * (GLA_DK ** -0.5)
        k_ref[...] = _wdot(xn_ref[...], wk_ref[...])

    def proj_v():
        v_ref[...] = _wdot(xn_ref[...], wv_ref[...]).astype(BF16)

    def proj_zg():
        zg = _wdot(xn_ref[...], wzg_ref[...])
        zg_ref[...] = zg * jax.nn.sigmoid(zg)

    def proj_g():
        al = _wdot(xn_ref[...], wal_ref[...])
        yield
        gpre = _wdot(al.astype(BF16), wa2_ref[...]) + ba_ref[...]
        g = jax.nn.log_sigmoid(gpre) * (LOG2E / GATE_NORM)
        g_ref[...] = g
        g1, g2, g3 = _split3(g)
        yield
        lseg = lseg_ref[...]
        b_ref[...] = _dot(lseg, g1) + _dot(lseg, g2) + _dot(lseg, g3)

    def proj_gl(half):
        cs = slice(half * D_MODEL, (half + 1) * D_MODEL)
        gl_ref[:, cs] = jax.nn.sigmoid(_wdot(xn_ref[...], wgl_ref[:, cs]) + bm_ref[:, cs])

    rowl = lax.broadcasted_iota(jnp.int32, (r_rows, LANES), 0)
    n_low = min(tt, SUBLANES).bit_length() - 1

    def gla_prep(hh):
        ks = slice(hh * GLA_DK, (hh + 1) * GLA_DK)
        qh, kh, gh, b = q_ref[:, ks], k_ref[:, ks], g_ref[:, ks], b_ref[:, ks]
        m = 1
        for lv in range(n_low):
            if m == 1:
                dd = jnp.where((rowl & 1) == 1, gh, 0.0)
            elif m == 2:
                c4 = rowl & 3
                dd = jnp.where(c4 == 0, pltpu.roll(gh, r_rows - 1, 0),
                               jnp.where(c4 == 2, gh,
                                         jnp.where(c4 == 3, gh + pltpu.roll(gh, 1, 0), 0.0)))
            else:
                dd = -jnp.abs(b - _seg_bcast(b, 2 * m, m - 1))
            zlow_ref[lv, :, ks] = (jnp.where((rowl & m) != 0, qh, kh) * jnp.exp2(dd)).astype(BF16)
            m *= 2
        lv = 0
        while m < tt:
            lows, ups = [], []
            for blk in range(r_rows // (2 * m)):
                r0 = blk * 2 * m
                bref = b[r0 + m - 1:r0 + m, :]
                lows.append(kh[r0:r0 + m] * jnp.exp2(bref - b[r0:r0 + m]))
                ups.append(qh[r0 + m:r0 + 2 * m] * jnp.exp2(b[r0 + m:r0 + 2 * m] - bref))
            zhi_ref[lv, :, ks] = jnp.concatenate(
                [p for pair in zip(lows, ups) for p in pair], axis=0).astype(BF16)
            zup_ref[lv, :, ks] = jnp.concatenate(ups, axis=0).astype(BF16)
            m *= 2
            lv += 1
        bl = _seg_bcast(b, tt, tt - 1)
        qe_ref[:, ks] = (qh * jnp.exp2(b)).astype(qe_ref.dtype)
        kl_ref[:, ks] = (kh * jnp.exp2(bl - b)).astype(kl_ref.dtype)

    def both(*fs):
        return lambda: [f() for f in fs]

    def proj_xl():
        xl_full = _wdot(xn_ref[...], wxl_ref[...])
        for n in range(LRU_BLOCKS):
            xl_ref[n] = xl_full[:, n * LRU_BW:(n + 1) * LRU_BW]

    sub = lax.broadcasted_iota(jnp.int32, (SUBLANES, LANES), 0)

    def lru_block(n):
        sl = slice(n * LRU_BW, (n + 1) * LRU_BW)
        cw = [cw_ref[j:j + 1, sl] for j in range(CONV_W)]
        if has_state:
            rowl = lax.broadcasted_iota(jnp.int32, (r_rows, LANES), 0)
            tpos = rowl & (tt - 1)
            xl = xl_ref[n]
            prev = spad_ref[0, :, sl]
            xc = cb_ref[:, sl] + cw[CONV_W - 1] * xl
            for kk in range(1, CONV_W):
                sh = jnp.where(tpos >= kk, pltpu.roll(xl, kk, 0),
                               pltpu.roll(prev, (kk - tt) % r_rows, 0))
                xc = xc + cw[CONV_W - 1 - kk] * sh
            yield
            a, u = _lru_gates(xc, n, wri_ref, br_ref, bi_ref, lam_ref)
            u = u + jnp.where(tpos == 0, a * prev, 0.0)
            _, h = _scan_sublanes(a, u, tpos, tt)
            h_ref[n] = h
            htail_ref[0, :, sl] = h
            xtail_ref[0, :, sl] = xl
        else:
            nw = r_rows // WINDOW
            xv = [[xl_ref[n, pl.ds(w * WINDOW + r, SUBLANES, stride=STEP), :] for r in range(STEP)]
                  for w in range(nw)]
            xtail = xtail_ref[0, :, sl]
            hprev = htail_ref[0, SUBLANES - 1:SUBLANES, sl]
            rolled = [[None] + [pltpu.roll(xv[w][q], 1, 0) for q in range(1, STEP)] for w in range(nw)]

            def shifted(w, r, kk):
                if r >= kk:
                    return xv[w][r - kk]
                q = r - kk + STEP
                before = rolled[w - 1][q] if w else xtail[SUBLANES + r - kk:SUBLANES + r - kk + 1, :]
                return jnp.where(sub == 0, before, rolled[w][q])

            xc = jnp.concatenate(
                [cb_ref[:, sl] + cw[CONV_W - 1] * xv[w][r]
                 + sum(cw[CONV_W - 1 - kk] * shifted(w, r, kk) for kk in range(1, CONV_W))
                 for w in range(nw) for r in range(STEP)], axis=0)
            yield
            a, u = _lru_gates(xc, n, wri_ref, br_ref, bi_ref, lam_ref)
            cprev = jnp.zeros((SUBLANES, LANES), F32)
            for w in range(nw):
                base = w * WINDOW
                av = [a[base + r * SUBLANES:base + (r + 1) * SUBLANES] for r in range(STEP)]
                uv = [u[base + r * SUBLANES:base + (r + 1) * SUBLANES] for r in range(STEP)]
                if w == 0:
                    uv[0] = uv[0] + jnp.where(sub == 0, av[0] * hprev, 0.0)
                hl, pr = [uv[0]], [av[0]]
                for r in range(1, STEP):
                    hl.append(av[r] * hl[-1] + uv[r])
                    pr.append(av[r] * pr[-1])
                pc, hc = _scan_sublanes(pr[-1], hl[-1], sub, SUBLANES)
                cend = hc + pc * cprev
                cin = jnp.where(sub == 0, cprev, pltpu.roll(cend, 1, 0))
                cprev = jnp.broadcast_to(cend[SUBLANES - 1:, :], (SUBLANES, LANES))
                for r in range(STEP):
                    h_ref[n, pl.ds(base + r, SUBLANES, stride=STEP), :] = hl[r] + pr[r] * cin
            htail_ref[0, :, sl] = h_ref[n, r_rows - SUBLANES:, :]
            xtail_ref[0, :, sl] = xl_ref[n, r_rows - SUBLANES:, :]

    def branch_a_out():
        hfull = jnp.concatenate([h_ref[n] for n in range(LRU_BLOCKS)], axis=1)
        ya = _wdot((hfull * zl_ref[...]).astype(BF16), wlo_ref[...])
        gl_ref[:, :D_MODEL] = gl_ref[:, :D_MODEL] * ya

    dblk = min(r_rows, LANES)
    def gla_head(hh):
        ks = slice(hh * GLA_DK, (hh + 1) * GLA_DK)
        vs = slice(hh * GLA_DV, (hh + 1) * GLA_DV)
        vh = v_ref[:, vs]
        for i in range(r_rows // dblk):
            rs = slice(i * dblk, (i + 1) * dblk)
            codeb = code_ref[rs, rs]
            attb = jnp.zeros((dblk, dblk), F32)
            for lv in range(n_low):
                z = zlow_ref[lv, rs, ks]
                attb = jnp.where(codeb == lv + 1, _dot_nt(z, z), attb)
            att_ref[hh, rs, rs] = attb
            for j in range(i + 1, r_rows // dblk):
                att_ref[hh, rs, j * dblk:(j + 1) * dblk] = jnp.zeros((dblk, dblk), F32)
        m = SUBLANES
        lv = 0
        while m < tt:
            prod = _dot_nt(zup_ref[lv, :, ks], zhi_ref[lv, :, ks])
            for blk in range(r_rows // (2 * m)):
                r0 = blk * 2 * m
                cv = (r0 // dblk) * dblk
                rr, cc = slice(r0 + m, r0 + 2 * m), slice(cv, cv + dblk)
                pblk = prod[blk * m:(blk + 1) * m, cc]
                if m == dblk:
                    att_ref[hh, rr, cc] = pblk
                else:
                    att_ref[hh, rr, cc] = jnp.where(code_ref[rr, cc] == n_low + lv + 1, pblk, att_ref[hh, rr, cc])
            m *= 2
            lv += 1
        yield
        dg = jnp.sum(q_ref[:, ks] * k_ref[:, ks], axis=-1, keepdims=True)
        o_intra = _dot(att_ref[hh].astype(BF16), vh) + dg * vh.astype(F32)

        b = b_ref[:, ks]
        if ns == 1:
            bt = jnp.transpose(b[r_rows - SUBLANES:, :])
        else:
            bt = jnp.transpose(b)
        for s in range(ns):
            rs = slice(s * tt, (s + 1) * tt)
            col = SUBLANES - 1 if ns == 1 else s * tt + tt - 1
            dcol = jnp.exp2(bt[:, col:col + 1])
            st = s0_ref[s, hh] if has_state else sout_ref[0, hh]
            oi_ref[rs, vs] = _dot(qe_ref[rs, ks].astype(BF16), st.astype(BF16))
            vrows = vh if ns == 1 else v_ref[:, vs].astype(F32)[rs].astype(BF16)
            snew = dcol * st + _dot_tn(kl_ref[rs, ks].astype(BF16), vrows)
            if has_state:
                sout_ref[s, hh] = snew
            else:
                sout_ref[0, hh] = snew
        o = o_intra + oi_ref[:, vs]
        on = o * lax.rsqrt(jnp.mean(o * o, axis=-1, keepdims=True) + EPS) * gng_ref[:, vs]
        ybin_ref[:, vs] = (on * zg_ref[:, vs]).astype(BF16)

    paused = {}

    def start(fn, *args):
        paused[(fn,) + args] = gen = fn(*args)
        next(gen)

    def finish(fn, *args):
        for _ in paused.pop((fn,) + args):
            pass

    lru, head = lru_block, gla_head
    start(proj_g)
    proj_xl()
    next(paused[(proj_g,)])
    proj_qk()
    start(lru, 0), start(lru, 1)
    finish(lru, 0)
    finish(proj_g)
    proj_v()
    finish(lru, 1)
    gla_prep(0)
    start(lru, 2), start(lru, 3)
    proj_zl()
    finish(lru, 2)
    start(head, 0)
    proj_zg()
    finish(lru, 3)
    gla_prep(1)
    start(lru, 4), start(lru, 5)
    finish(head, 0)
    proj_gl(0)
    finish(lru, 4)
    start(head, 1)
    gla_prep(2)
    proj_gl(1)
    finish(lru, 5)
    finish(head, 1)
    start(lru, 6), start(lru, 7)
    start(head, 2)
    finish(lru, 6)
    gla_prep(3)
    finish(lru, 7)
    start(head, 3)
    finish(head, 2)
    branch_a_out()
    finish(head, 3)
    yb = _wdot(ybin_ref[...], wgo_ref[...])

    merged = gl_ref[:, :D_MODEL] + gl_ref[:, D_MODEL:] * yb
    out = x_ref[0] + _wdot(merged.astype(BF16), wout_ref[...])
    ms2 = jnp.mean(out * out, axis=-1, keepdims=True)
    y_ref[0] = out * lax.rsqrt(ms2 + EPS) * fng_ref[...]


def _resident(shape):
    nd = len(shape)
    return pl.BlockSpec(shape, lambda *_: (0,) * nd, pipeline_mode=pl.Buffered(1))


def _run_group(x, weights, ns, tt, nk=1, state=None):
    has_state = state is not None
    r_rows = ns * tt
    groups, rows_total, _ = x.shape
    nt = rows_total // (nk * r_rows)
    tile = lambda c: pl.BlockSpec((1, nk * r_rows, c), lambda i, j: (i, j, 0))
    in_specs = [tile(D_MODEL)]
    args = [x]
    if has_state:
        spad, s0 = state
        in_specs += [tile(W_LRU),
                     pl.BlockSpec((ns, GLA_HEADS, GLA_DK, GLA_DV), lambda i, j: (i, 0, 0, 0))]
        args += [spad, s0]
    in_specs += [_resident(w.shape) for w in weights]
    args += list(weights)

    tail = (1, ns * SUBLANES, W_LRU)
    s_blk = (ns, GLA_HEADS, GLA_DK, GLA_DV) if has_state else (1, GLA_HEADS, GLA_DK, GLA_DV)
    out_shape = (
        jax.ShapeDtypeStruct(x.shape, F32),
        jax.ShapeDtypeStruct((groups,) + tail[1:], F32),
        jax.ShapeDtypeStruct((groups,) + tail[1:], F32),
        jax.ShapeDtypeStruct((groups * s_blk[0],) + s_blk[1:], F32),
    )
    out_specs = (
        tile(D_MODEL),
        pl.BlockSpec(tail, lambda i, j: (i, 0, 0)),
        pl.BlockSpec(tail, lambda i, j: (i, 0, 0)),
        pl.BlockSpec(s_blk, lambda i, j: (i, 0, 0, 0)),
    )
    hk = GLA_HEADS * GLA_DK
    hv = GLA_HEADS * GLA_DV
    n_low = min(tt, SUBLANES).bit_length() - 1
    n_high = tt.bit_length() - 1 - n_low
    state_operand = BF16 if ns == 1 else F32
    scratch = [
        pltpu.VMEM((r_rows, D_MODEL), BF16),
        pltpu.VMEM((LRU_BLOCKS, r_rows, LRU_BW), F32),
        pltpu.VMEM((LRU_BLOCKS, r_rows, LRU_BW), F32),
        pltpu.VMEM((r_rows, W_LRU), F32),
        pltpu.VMEM((r_rows, hk), F32),
        pltpu.VMEM((r_rows, hk), F32),
        pltpu.VMEM((r_rows, hk), F32),
        pltpu.VMEM((r_rows, hk), F32),
        pltpu.VMEM((r_rows, hv), BF16),
        pltpu.VMEM((r_rows, hv), F32),
        pltpu.VMEM((r_rows, 2 * D_MODEL), F32),
        pltpu.VMEM((r_rows, hv), F32),
        pltpu.VMEM((r_rows, r_rows), BF16),
        pltpu.VMEM((r_rows, r_rows), jnp.int32),
        pltpu.VMEM((GLA_HEADS, r_rows, r_rows), F32),
        pltpu.VMEM((n_low, r_rows, hk), BF16),
        pltpu.VMEM((max(n_high, 1), r_rows, hk), BF16),
        pltpu.VMEM((max(n_high, 1), r_rows // 2, hk), BF16),
        pltpu.VMEM((r_rows, hk), state_operand),
        pltpu.VMEM((r_rows, hk), state_operand),
        pltpu.VMEM((r_rows, hv), BF16),
    ]
    assert len(scratch) == N_SCRATCH and len(out_specs) == N_OUT
    return pl.pallas_call(
        functools.partial(_mixer_kernel, ns, tt, nk, has_state),
        grid=(groups, nt),
        in_specs=in_specs,
        out_specs=out_specs,
        out_shape=out_shape,
        scratch_shapes=scratch,
        compiler_params=pltpu.CompilerParams(
            dimension_semantics=("arbitrary", "arbitrary"),
            vmem_limit_bytes=VMEM_LIMIT),
        name="mixer_sample" if has_state else "mixer_prompt",
    )(*args)


def _pack_kernel(win_ref, wlo_ref, wgo_ref, wout_ref, wr_ref, wi_ref, wa2_ref,
                 oxl, ozl, oq, ok, ov, ozg, oal, ogl, olo, ogo, oout, ori, oa2):
    pk = lambda v: pltpu.bitcast(v.astype(BF16), jnp.uint32)
    offs = [0]
    for sz in IN_SIZES:
        offs.append(offs[-1] + sz)
    for o_ref, i in ((oxl, 0), (ozl, 1), (oq, 2), (ok, 3), (ov, 4), (ozg, 5), (ogl, 7)):
        o_ref[...] = pk(win_ref[:, offs[i]:offs[i + 1]])
    lane = lax.broadcasted_iota(jnp.int32, (PACK_ROWS, LANES), 1)
    oal[...] = pk(jnp.where(lane < GATE_RANK, win_ref[:, offs[6]:offs[6] + LANES], 0.0))
    olo[...] = pk(wlo_ref[...])
    ogo[...] = pk(wgo_ref[...])
    oout[...] = pk(wout_ref[...])
    ori[0] = pk(jnp.concatenate([wr_ref[0], wi_ref[0]], axis=1))
    oa2[...] = pk(jnp.concatenate(
        [wa2_ref[...], jnp.zeros((LANES - GATE_RANK, wa2_ref.shape[1]), F32)], axis=0))


def _pack_weights(w_in, w_lru_o, w_gla_o, w_out, w_r, w_i, w_a2):
    steps = D_MODEL // PACK_ROWS
    assert steps == LRU_BLOCKS and LRU_BW == PACK_ROWS
    half = PACK_ROWS // 2
    n_in = w_in.shape[-1]
    rows_in = lambda n: pl.BlockSpec((None, PACK_ROWS, n), lambda i: (0, i, 0))
    rows_out = lambda n: pl.BlockSpec((half, n), lambda i: (i, 0))
    gate_in = pl.BlockSpec((None, 1, LRU_BW, LRU_BW), lambda i: (0, i, 0, 0))
    hk = GLA_HEADS * GLA_DK
    widths = [IN_SIZES[0], IN_SIZES[1], IN_SIZES[2], IN_SIZES[3], IN_SIZES[4], IN_SIZES[5],
              LANES, IN_SIZES[7], D_MODEL, D_MODEL, D_MODEL]
    u32 = lambda *shape: jax.ShapeDtypeStruct(shape, jnp.uint32)
    return pl.pallas_call(
        _pack_kernel,
        grid=(steps,),
        in_specs=[rows_in(n_in), rows_in(D_MODEL), rows_in(D_MODEL), rows_in(D_MODEL), gate_in, gate_in,
                  pl.BlockSpec((None, GATE_RANK, hk), lambda i: (0, 0, 0))],
        out_specs=[rows_out(n) for n in widths]
        + [pl.BlockSpec((1, half, 2 * LRU_BW), lambda i: (i, 0, 0)),
           pl.BlockSpec((LANES // 2, hk), lambda i: (0, 0))],
        out_shape=[u32(D_MODEL // 2, n) for n in widths]
        + [u32(LRU_BLOCKS, half, 2 * LRU_BW), u32(LANES // 2, hk)],
        compiler_params=pltpu.CompilerParams(
            dimension_semantics=("arbitrary",), vmem_limit_bytes=PACK_VMEM_LIMIT),
        name="pack_weights",
    )(w_in, w_lru_o, w_gla_o, w_out, w_r, w_i, w_a2)


def kernel(x_prompt, x_sample, state_lru_h, state_lru_conv, state_gla, norm_g, w_in, conv_w, conv_b,
           w_r, b_r, w_i, b_i, lam, w_a2, b_a, gla_norm_g, w_lru_o, w_gla_o, b_merge, w_out, final_norm_g):
    assert norm_g.shape[0] == 1, "single-layer trunk"
    bp, seq, _ = x_prompt.shape
    bs, dseq, _ = x_sample.shape
    assert seq % (PROMPT_TILE * PROMPT_TILES_PER_STEP) == 0 and dseq == SUBLANES and bs % SAMPLE_SEQS == 0

    row = lambda p: p[0].reshape(1, -1).astype(F32)
    (wxl, wzl, wq, wk, wv, wzg, wal, wgl, wlo, wgo, wout, wri, wa2p) = _pack_weights(
        w_in, w_lru_o, w_gla_o, w_out, w_r, w_i, w_a2)
    weights = (
        row(norm_g), wxl, wzl, wq, wk, wv, wzg, wal, wgl,
        conv_w[0].astype(F32), row(conv_b), wri, row(b_r), row(b_i),
        row(lam), wa2p, row(b_a), row(gla_norm_g), wlo, wgo,
        row(b_merge), wout, final_norm_g.reshape(1, -1).astype(F32),
    )

    yp, hp, cp, sp = _run_group(x_prompt, weights, 1, PROMPT_TILE, nk=PROMPT_TILES_PER_STEP)

    ns = SAMPLE_SEQS
    spad = jnp.concatenate(
        [state_lru_h[0][:, None, :],
         jnp.zeros((bs, SUBLANES - CONV_W, W_LRU), F32),
         state_lru_conv[0].astype(F32)], axis=1)
    ys, hs, cs, ss = _run_group(
        x_sample.reshape(bs // ns, ns * dseq, D_MODEL), weights, ns, dseq,
        state=(spad.reshape(bs // ns, ns * dseq, W_LRU), state_gla[0]))

    tailp = lambda t: t.reshape(bp, SUBLANES, W_LRU)
    tails = lambda t: t.reshape(bs, SUBLANES, W_LRU)
    keep = SUBLANES - (CONV_W - 1)
    return (yp, ys.reshape(bs, dseq, D_MODEL),
            tailp(hp)[None, :, SUBLANES - 1], tailp(cp)[None, :, keep:], sp[None],
            tails(hs)[None, :, SUBLANES - 1], tails(cs)[None, :, keep:], ss[None])
```
